```python
import jax, jax.numpy as jnp
from jax import lax
import numpy as np

D_MODEL = 4096
BATCH = 1
SEQ = 8192
DEPTH = 1
DEC_BATCH = 32
DEC_SEQ = 32
PAST_LEN = 2048

CHUNK = 64
W_CONV = D_MODEL // 2
CONV_K = 31
W_POOL = D_MODEL // 2
POOL_WINDOWS = (2, 4, 8, 16)
N_POOL_GROUPS = len(POOL_WINDOWS)
POOL_GC = W_POOL // N_POOL_GROUPS
POOL_GO = D_MODEL // N_POOL_GROUPS
POOL_MAX = max(POOL_WINDOWS)
D_FF = ((8 * D_MODEL // 3 + 255) // 256) * 256
D_IN = 2 * W_CONV + W_POOL + 2 * D_MODEL
EPS = 1e-6

kernel_name = "gated_conv_pool_streaming_encoder"


def _rmsnorm(x, g):
    xf = x.astype(jnp.float32)
    y = xf * lax.rsqrt(jnp.mean(xf * xf, axis=-1, keepdims=True) + EPS)
    return (y * g.astype(jnp.float32)).astype(x.dtype)


def _layernorm(x, g, b):
    xf = x.astype(jnp.float32)
    mu = jnp.mean(xf, axis=-1, keepdims=True)
    var = jnp.mean(jnp.square(xf - mu), axis=-1, keepdims=True)
    y = (xf - mu) * lax.rsqrt(var + EPS)
    return (y * g.astype(jnp.float32) + b.astype(jnp.float32)).astype(x.dtype)


def _depthwise_causal_conv(full, w, b):
    c = full.shape[-1]
    out = lax.conv_general_dilated(
        full, w[:, None, :].astype(full.dtype), window_strides=(1,), padding="VALID",
        dimension_numbers=("NWC", "WIO", "NWC"), feature_group_count=c)
    return out + b.astype(full.dtype)


def _multiscale_pool(full, t_new, pos0):
    dt = full.dtype
    ff = full.astype(jnp.float32)
    cs = jnp.cumsum(ff, axis=1)
    cs = jnp.concatenate([jnp.zeros_like(cs[:, :1]), cs], axis=1)
    pos = (pos0 + jnp.arange(t_new)).astype(jnp.float32)[None, :, None]
    x_tok = ff[:, POOL_MAX - 1:]
    outs = []
    for g, w in enumerate(POOL_WINDOWS):
        sl = slice(g * POOL_GC, (g + 1) * POOL_GC)
        s = cs[:, POOL_MAX:POOL_MAX + t_new, sl] - cs[:, POOL_MAX - w:POOL_MAX - w + t_new, sl]
        cnt = jnp.minimum(jnp.float32(w), pos + 1.0)
        outs.append(s / cnt - x_tok[..., sl])
    return jnp.concatenate(outs, axis=-1).astype(dt)


def _layer(h, c, conv_hist, pool_hist, pos0, w_ada, b_ada, g_norm1, w_in, conv_w, conv_b,
           ln_g, ln_b, w_conv_out, w_pool, pool_scale, w_out, g_norm2, w_ffn_in, w_ffn_out):
    bsz, t_new, _ = h.shape
    mod = (jax.nn.silu(c) @ w_ada + b_ada)[:, None, :]
    shift1, scale1, gate1, shift2, scale2, gate2 = jnp.split(mod, 6, axis=-1)

    u = _rmsnorm(h, g_norm1) * (1 + scale1) + shift1
    z = u @ w_in
    za, zg, zp, ga, gb = jnp.split(
        z, [W_CONV, 2 * W_CONV, 2 * W_CONV + W_POOL, 2 * W_CONV + W_POOL + D_MODEL], axis=-1)

    a_in = za * jax.nn.sigmoid(zg)
    conv_full = jnp.concatenate([conv_hist.astype(a_in.dtype), a_in], axis=1)
    a = _depthwise_causal_conv(conv_full, conv_w, conv_b)
    a = jax.nn.silu(_layernorm(a, ln_g, ln_b)) @ w_conv_out

    pool_full = jnp.concatenate([pool_hist.astype(zp.dtype), zp], axis=1)
    p = _multiscale_pool(pool_full, t_new, pos0)
    p = jnp.einsum("btgc,gcd->btgd", p.reshape(bsz, t_new, N_POOL_GROUPS, POOL_GC), w_pool)
    p = p.reshape(bsz, t_new, D_MODEL) * pool_scale

    m = jax.nn.sigmoid(ga) * a + jax.nn.sigmoid(gb) * p
    h = h + gate1 * (m @ w_out)

    u2 = _rmsnorm(h, g_norm2) * (1 + scale2) + shift2
    gu = u2 @ w_ffn_in
    f_g, f_u = jnp.split(gu, 2, axis=-1)
    h = h + gate2 * ((jax.nn.silu(f_g) * f_u) @ w_ffn_out)

    new_conv = conv_full[:, -(CONV_K - 1):]
    new_pool = pool_full[:, -(POOL_MAX - 1):]
    return h, new_conv, new_pool


def setup_inputs(seed: int = 0) -> dict:
    key = jax.random.key(seed)
    ks = jax.random.split(key, 24)
    nrm = lambda k, shape, s: jax.random.normal(k, shape, jnp.float32) * s
    L = DEPTH
    return {
        "x_prompt": nrm(ks[0], (BATCH, SEQ, D_MODEL), 1.0),
        "x_sample": nrm(ks[1], (DEC_BATCH, DEC_SEQ, D_MODEL), 1.0),
        "state_conv": nrm(ks[2], (L, DEC_BATCH, CONV_K - 1, W_CONV), 0.5),
        "state_pool": nrm(ks[3], (L, DEC_BATCH, POOL_MAX - 1, W_POOL), 1.0),
        "c_prompt": nrm(ks[4], (BATCH, D_MODEL), 1.0),
        "c_sample": nrm(ks[5], (DEC_BATCH, D_MODEL), 1.0),
        "w_ada": nrm(ks[6], (L, D_MODEL, 6 * D_MODEL), 0.2 * D_MODEL ** -0.5),
        "b_ada": nrm(ks[7], (L, 6 * D_MODEL), 0.02),
        "g_norm1": 1.0 + nrm(ks[8], (L, D_MODEL), 0.02),
        "w_in": nrm(ks[9], (L, D_MODEL, D_IN), D_MODEL ** -0.5),
        "conv_w": nrm(ks[10], (L, CONV_K, W_CONV), CONV_K ** -0.5),
        "conv_b": nrm(ks[11], (L, W_CONV), 0.02),
        "ln_g": 1.0 + nrm(ks[12], (L, W_CONV), 0.02),
        "ln_b": nrm(ks[13], (L, W_CONV), 0.02),
        "w_conv_out": nrm(ks[14], (L, W_CONV, D_MODEL), W_CONV ** -0.5),
        "w_pool": nrm(ks[15], (L, N_POOL_GROUPS, POOL_GC, POOL_GO), POOL_GC ** -0.5),
        "pool_scale": 1.0 + nrm(ks[16], (L, D_MODEL), 0.02),
        "w_out": nrm(ks[17], (L, D_MODEL, D_MODEL), D_MODEL ** -0.5),
        "g_norm2": 1.0 + nrm(ks[18], (L, D_MODEL), 0.02),
        "w_ffn_in": nrm(ks[19], (L, D_MODEL, 2 * D_FF), D_MODEL ** -0.5),
        "w_ffn_out": nrm(ks[20], (L, D_FF, D_MODEL), D_FF ** -0.5),
        "g_final": 1.0 + nrm(ks[21], (D_MODEL,), 0.02),
    }


def reference(x_prompt, x_sample, state_conv, state_pool, c_prompt, c_sample, w_ada, b_ada,
              g_norm1, w_in, conv_w, conv_b, ln_g, ln_b, w_conv_out, w_pool, pool_scale,
              w_out, g_norm2, w_ffn_in, w_ffn_out, g_final):
    hp, hs = x_prompt, x_sample
    bp = x_prompt.shape[0]
    conv_p, pool_p, conv_s, pool_s = [], [], [], []
    for l in range(DEPTH):
        params = (w_ada[l], b_ada[l], g_norm1[l], w_in[l], conv_w[l], conv_b[l], ln_g[l], ln_b[l],
                  w_conv_out[l], w_pool[l], pool_scale[l], w_out[l], g_norm2[l], w_ffn_in[l],
                  w_ffn_out[l])
        zc = jnp.zeros((bp, CONV_K - 1, W_CONV), hp.dtype)
        zp = jnp.zeros((bp, POOL_MAX - 1, W_POOL), hp.dtype)
        hp, nc, npl = _layer(hp, c_prompt, zc, zp, 0, *params)
        conv_p.append(nc)
        pool_p.append(npl)
        hs, nc, npl = _layer(hs, c_sample, state_conv[l], state_pool[l], PAST_LEN, *params)
        conv_s.append(nc)
        pool_s.append(npl)
    y_prompt = _rmsnorm(hp, g_final)
    y_sample = _rmsnorm(hs, g_final)
    new_state_conv_prompt = jnp.stack(conv_p)
    new_state_pool_prompt = jnp.stack(pool_p)
    new_state_conv_sample = jnp.stack(conv_s)
    new_state_pool_sample = jnp.stack(pool_s)
    return (y_prompt, y_sample, new_state_conv_prompt, new_state_pool_prompt,
            new_state_conv_sample, new_state_pool_sample)
```

```python
import functools

import jax
import jax.numpy as jnp
from jax import lax
from jax.experimental import pallas as pl
from jax.experimental.pallas import tpu as pltpu

EPS = 1e-6
CONV_K = 31
POOL_WINDOWS = (2, 4, 8, 16)
POOL_MAX = max(POOL_WINDOWS)
PAST_LEN = 2048
HALO = 32
VMEM_LIMIT = 56 * 1024 * 1024

BF16 = jnp.bfloat16
F32 = jnp.float32


def _params(n_axes, vmem=VMEM_LIMIT):
    return pltpu.CompilerParams(dimension_semantics=("arbitrary",) * n_axes,
                                vmem_limit_bytes=vmem)


def _dot(a, b):
    return jnp.dot(a, b, preferred_element_type=F32)


def _sigmoid(x):
    return jax.nn.sigmoid(x)


def _ada_kernel(c_ref, w_ref, b_ref, o_ref):
    c = c_ref[...]
    s = (c * _sigmoid(c)).astype(BF16)
    o_ref[...] = _dot(s, w_ref[...].astype(BF16)) + b_ref[...]


def _ada(c, w, b, tn=512):
    rows, d = c.shape
    n = w.shape[1]
    return pl.pallas_call(
        _ada_kernel,
        grid=(n // tn,),
        in_specs=[pl.BlockSpec((rows, d), lambda j: (0, 0)),
                  pl.BlockSpec((d, tn), lambda j: (0, j)),
                  pl.BlockSpec((1, tn), lambda j: (0, j))],
        out_specs=pl.BlockSpec((rows, tn), lambda j: (0, j)),
        out_shape=jax.ShapeDtypeStruct((rows, n), F32),
        compiler_params=_params(1),
        name="ada_proj",
    )(c, w, b.reshape(1, n))


def _norm_mod_kernel(h_ref, g_ref, sc_ref, sh_ref, o_ref):
    x = h_ref[...]
    ms = jnp.mean(x * x, axis=-1, keepdims=True)
    y = x * lax.rsqrt(ms + EPS) * g_ref[...]
    o_ref[...] = (y * (1.0 + sc_ref[...]) + sh_ref[...]).astype(o_ref.dtype)


def _norm_kernel(h_ref, g_ref, o_ref):
    x = h_ref[...]
    ms = jnp.mean(x * x, axis=-1, keepdims=True)
    o_ref[...] = (x * lax.rsqrt(ms + EPS) * g_ref[...]).astype(o_ref.dtype)


def _norm(h, g, mod, chunks, bb, tt, out_dtype):
    B, T, D = h.shape
    grid = (B // bb, T // tt)
    h_spec = pl.BlockSpec((bb, tt, D), lambda b, t: (b, t, 0))
    g_spec = pl.BlockSpec((1, 1, D), lambda b, t: (0, 0, 0))
    if mod is None:
        kern, in_specs, args = _norm_kernel, [h_spec, g_spec], (h, g.reshape(1, 1, D))
    else:
        sc, sh = chunks
        kern = _norm_mod_kernel
        in_specs = [h_spec, g_spec,
                    pl.BlockSpec((bb, 1, D), lambda b, t: (b, 0, sc)),
                    pl.BlockSpec((bb, 1, D), lambda b, t: (b, 0, sh))]
        args = (h, g.reshape(1, 1, D), mod, mod)
    return pl.pallas_call(
        kern, grid=grid, in_specs=in_specs, out_specs=h_spec,
        out_shape=jax.ShapeDtypeStruct((B, T, D), out_dtype),
        compiler_params=_params(2), name="rmsnorm",
    )(*args)


def _mm_kernel(x_ref, w_ref, o_ref):
    o_ref[...] = _dot(x_ref[...], w_ref[...].astype(BF16)).astype(o_ref.dtype)


def _matmul(x, w, tm, tn, out_dtype):
    R, K = x.shape
    N = w.shape[1]
    return pl.pallas_call(
        _mm_kernel,
        grid=(R // tm, N // tn),
        in_specs=[pl.BlockSpec((tm, K), lambda i, j: (i, 0)),
                  pl.BlockSpec((K, tn), lambda i, j: (0, j))],
        out_specs=pl.BlockSpec((tm, tn), lambda i, j: (i, j)),
        out_shape=jax.ShapeDtypeStruct((R, N), out_dtype),
        compiler_params=_params(2), name="in_proj",
    )(x, w)


def _mixer_kernel(za_ref, zg_ref, zp_ref, zah_ref, zgh_ref, zph_ref, hc_ref, hp_ref,
                  cw_ref, cb_ref, lg_ref, lb_ref,
                  a_ref, p_ref, nc_ref, np_ref,
                  cfull, pfull, conv_out, *, pos0, pool_gc):
    bb, tt, C = za_ref.shape
    t = pl.program_id(1)
    nt = pl.num_programs(1)

    first = t == 0
    a_hist = zah_ref[...] * _sigmoid(zgh_ref[...])
    cfull[:, 0:HALO, :] = jnp.where(first, hc_ref[...], a_hist)
    pfull[:, 0:HALO, :] = jnp.where(first, hp_ref[...], zph_ref[...])
    cfull[:, HALO:HALO + tt, :] = za_ref[...] * _sigmoid(zg_ref[...])
    pfull[:, HALO:HALO + tt, :] = zp_ref[...]

    off = HALO - (CONV_K - 1)
    rows = min(tt, 64)
    lanes = 128
    for c in range(C // lanes):
        cs = slice(c * lanes, (c + 1) * lanes)
        wk = [cw_ref[k:k + 1, cs].reshape(1, 1, lanes) for k in range(CONV_K)]
        bias = cb_ref[:, cs].reshape(1, 1, lanes)
        for r in range(tt // rows):
            acc = jnp.broadcast_to(bias, (bb, rows, lanes))
            for k in range(CONV_K):
                lo = off + k + r * rows
                acc = acc + cfull[:, lo:lo + rows, cs] * wk[k]
            conv_out[:, r * rows:(r + 1) * rows, cs] = acc

    x = conv_out[...]
    mu = jnp.mean(x, axis=-1, keepdims=True)
    xc = x - mu
    var = jnp.mean(xc * xc, axis=-1, keepdims=True)
    y = xc * lax.rsqrt(var + EPS) * lg_ref[...] + lb_ref[...]
    a_ref[...] = (y * _sigmoid(y)).astype(a_ref.dtype)

    pos = (pos0 + t * tt + lax.broadcasted_iota(jnp.int32, (1, tt, 1), 1)).astype(F32)
    for g, w in enumerate(POOL_WINDOWS):
        gs = slice(g * pool_gc, (g + 1) * pool_gc)
        tok = pfull[:, HALO:HALO + tt, gs]
        s = tok
        for j in range(1, w):
            s = s + pfull[:, HALO - j:HALO - j + tt, gs]
        inv = 1.0 / jnp.minimum(jnp.float32(w), pos + 1.0)
        p_ref[:, :, gs] = (s * inv - tok).astype(p_ref.dtype)

    @pl.when(t == nt - 1)
    def _():
        nc_ref[...] = cfull[:, HALO + tt - (CONV_K - 1):HALO + tt, :]
        np_ref[...] = pfull[:, HALO + tt - (POOL_MAX - 1):HALO + tt, :]


def _mixer(z, hist_conv, hist_pool, conv_w, conv_b, ln_g, ln_b, pos0, bb, tt):
    B, T, _ = z.shape
    C = conv_w.shape[1]
    n_hc, n_hp = hist_conv.shape[1], hist_pool.shape[1]
    hc = jnp.pad(hist_conv, ((0, 0), (HALO - n_hc, 0), (0, 0)))
    hp = jnp.pad(hist_pool, ((0, 0), (HALO - n_hp, 0), (0, 0)))
    r = tt // HALO

    def cur(col):
        return pl.BlockSpec((bb, tt, C), lambda b, t: (b, t, col))

    def halo(col):
        return pl.BlockSpec((bb, HALO, C), lambda b, t: (b, jnp.maximum(t * r - 1, 0), col))

    hist = pl.BlockSpec((bb, HALO, C), lambda b, t: (b, 0, 0))
    vec = pl.BlockSpec((1, C), lambda b, t: (0, 0))
    kern = functools.partial(_mixer_kernel, pos0=pos0, pool_gc=C // len(POOL_WINDOWS))
    return pl.pallas_call(
        kern,
        grid=(B // bb, T // tt),
        in_specs=[cur(0), cur(1), cur(2), halo(0), halo(1), halo(2), hist, hist,
                  pl.BlockSpec((CONV_K, C), lambda b, t: (0, 0)), vec, vec, vec],
        out_specs=[pl.BlockSpec((bb, tt, C), lambda b, t: (b, t, 0)),
                   pl.BlockSpec((bb, tt, C), lambda b, t: (b, t, 0)),
                   pl.BlockSpec((bb, n_hc, C), lambda b, t: (b, 0, 0)),
                   pl.BlockSpec((bb, n_hp, C), lambda b, t: (b, 0, 0))],
        out_shape=[jax.ShapeDtypeStruct((B, T, C), BF16),
                   jax.ShapeDtypeStruct((B, T, C), BF16),
                   jax.ShapeDtypeStruct((B, n_hc, C), F32),
                   jax.ShapeDtypeStruct((B, n_hp, C), F32)],
        scratch_shapes=[pltpu.VMEM((bb, HALO + tt, C), F32),
                        pltpu.VMEM((bb, HALO + tt, C), F32),
                        pltpu.VMEM((bb, tt, C), F32)],
        compiler_params=_params(2), name="mixer",
    )(z, z, z, z, z, z, hc, hp, conv_w, conv_b.reshape(1, C), ln_g.reshape(1, C),
      ln_b.reshape(1, C))


def _merge_kernel(a_ref, p_ref, wc_ref, wp_ref, ps_ref, ga_ref, gb_ref, o_ref):
    a = _dot(a_ref[...], wc_ref[...].astype(BF16))
    p = _dot(p_ref[...], wp_ref[0].astype(BF16)) * ps_ref[...]
    o_ref[...] = (_sigmoid(ga_ref[...]) * a + _sigmoid(gb_ref[...]) * p).astype(o_ref.dtype)


def _merge(a_act, p, z, w_conv_out, w_pool, pool_scale, tm, tn):
    R, C = a_act.shape
    G, gc, go = w_pool.shape
    D = w_conv_out.shape[1]
    per = go // tn
    ga0 = (z.shape[1] - 2 * D) // tn
    gb0 = ga0 + D // tn
    return pl.pallas_call(
        _merge_kernel,
        grid=(R // tm, D // tn),
        in_specs=[pl.BlockSpec((tm, C), lambda i, j: (i, 0)),
                  pl.BlockSpec((tm, gc), lambda i, j: (i, j // per)),
                  pl.BlockSpec((C, tn), lambda i, j: (0, j)),
                  pl.BlockSpec((1, gc, tn), lambda i, j: (j // per, 0, j % per)),
                  pl.BlockSpec((1, tn), lambda i, j: (0, j)),
                  pl.BlockSpec((tm, tn), lambda i, j: (i, ga0 + j)),
                  pl.BlockSpec((tm, tn), lambda i, j: (i, gb0 + j))],
        out_specs=pl.BlockSpec((tm, tn), lambda i, j: (i, j)),
        out_shape=jax.ShapeDtypeStruct((R, D), BF16),
        compiler_params=_params(2), name="merge",
    )(a_act, p, w_conv_out, w_pool, pool_scale.reshape(1, D), z, z)


def _mm_res_kernel(x_ref, w_ref, h_ref, gate_ref, o_ref):
    acc = _dot(x_ref[...], w_ref[...].astype(BF16))
    o_ref[...] = h_ref[...] + gate_ref[...] * acc.reshape(h_ref.shape)


def _matmul_residual(x, w, h, mod, gate_chunk, bb, tt, tn, name):
    B, T, D = h.shape
    assert bb == 1 or tt == T
    K = x.shape[1]
    nt = T // tt
    g0 = gate_chunk * (D // tn)
    return pl.pallas_call(
        _mm_res_kernel,
        grid=(B // bb, nt, D // tn),
        in_specs=[pl.BlockSpec((bb * tt, K), lambda b, t, j: (b * nt + t, 0)),
                  pl.BlockSpec((K, tn), lambda b, t, j: (0, j)),
                  pl.BlockSpec((bb, tt, tn), lambda b, t, j: (b, t, j)),
                  pl.BlockSpec((bb, 1, tn), lambda b, t, j: (b, 0, g0 + j))],
        out_specs=pl.BlockSpec((bb, tt, tn), lambda b, t, j: (b, t, j)),
        out_shape=jax.ShapeDtypeStruct((B, T, D), F32),
        compiler_params=_params(3), name=name,
    )(x, w, h, mod)


def _ffn_in_kernel(x_ref, wg_ref, wu_ref, o_ref):
    x = x_ref[...]
    g = _dot(x, wg_ref[...].astype(BF16))
    u = _dot(x, wu_ref[...].astype(BF16))
    o_ref[...] = (g * _sigmoid(g) * u).astype(o_ref.dtype)


def _ffn_in(x, w, tm, tn):
    R, K = x.shape
    F = w.shape[1] // 2
    nf = F // tn
    return pl.pallas_call(
        _ffn_in_kernel,
        grid=(R // tm, nf),
        in_specs=[pl.BlockSpec((tm, K), lambda i, j: (i, 0)),
                  pl.BlockSpec((K, tn), lambda i, j: (0, j)),
                  pl.BlockSpec((K, tn), lambda i, j: (0, nf + j))],
        out_specs=pl.BlockSpec((tm, tn), lambda i, j: (i, j)),
        out_shape=jax.ShapeDtypeStruct((R, F), BF16),
        compiler_params=_params(2), name="ffn_in",
    )(x, w, w)


def _layer(h, mod, hist_conv, hist_pool, pos0, lw, *, bb, tt, ebb, ett, obb, ott):
    (g_norm1, w_in, conv_w, conv_b, ln_g, ln_b, w_conv_out, w_pool, pool_scale, w_out,
     g_norm2, w_ffn_in, w_ffn_out) = lw
    B, T, D = h.shape
    R = B * T
    tm = bb * tt

    u = _norm(h, g_norm1, mod, (1, 0), ebb, ett, BF16)
    z = _matmul(u.reshape(R, D), w_in, tm, 512, F32)
    a_act, p, new_conv, new_pool = _mixer(z.reshape(B, T, -1), hist_conv, hist_pool, conv_w,
                                          conv_b, ln_g, ln_b, pos0, ebb, ett)
    C = a_act.shape[-1]
    m = _merge(a_act.reshape(R, C), p.reshape(R, C), z, w_conv_out, w_pool, pool_scale, tm, 512)
    h = _matmul_residual(m, w_out, h, mod, 2, bb, tt, 512, "out_proj")

    u2 = _norm(h, g_norm2, mod, (4, 3), ebb, ett, BF16)
    act = _ffn_in(u2.reshape(R, D), w_ffn_in, tm, 256)
    h = _matmul_residual(act, w_ffn_out, h, mod, 5, obb, ott, 256, "ffn_out")
    return h, new_conv, new_pool


def kernel(x_prompt, x_sample, state_conv, state_pool, c_prompt, c_sample, w_ada, b_ada,
           g_norm1, w_in, conv_w, conv_b, ln_g, ln_b, w_conv_out, w_pool, pool_scale,
           w_out, g_norm2, w_ffn_in, w_ffn_out, g_final):
    depth = w_ada.shape[0]
    bp, _, D = x_prompt.shape
    bs, ts, _ = x_sample.shape

    c_all = jnp.concatenate([c_prompt, c_sample], axis=0)
    rows = c_all.shape[0]
    c_all = jnp.pad(c_all, ((0, -rows % 16), (0, 0)))

    hp, hs = x_prompt, x_sample
    conv_p, pool_p, conv_s, pool_s = [], [], [], []
    for l in range(depth):
        lw = (g_norm1[l], w_in[l], conv_w[l], conv_b[l], ln_g[l], ln_b[l], w_conv_out[l],
              w_pool[l], pool_scale[l], w_out[l], g_norm2[l], w_ffn_in[l], w_ffn_out[l])
        mod = _ada(c_all, w_ada[l], b_ada[l])
        mod_p = mod[:bp].reshape(bp, 1, -1)
        mod_s = mod[bp:bp + bs].reshape(bs, 1, -1)
        zc = jnp.zeros((bp, CONV_K - 1, conv_w.shape[-1]), F32)
        zp = jnp.zeros((bp, POOL_MAX - 1, conv_w.shape[-1]), F32)
        hp, nc, npl = _layer(hp, mod_p, zc, zp, 0, lw,
                             bb=1, tt=1024, ebb=1, ett=256, obb=1, ott=512)
        conv_p.append(nc)
        pool_p.append(npl)
        hs, nc, npl = _layer(hs, mod_s, state_conv[l], state_pool[l], PAST_LEN, lw,
                             bb=bs, tt=ts, ebb=4, ett=ts, obb=16, ott=ts)
        conv_s.append(nc)
        pool_s.append(npl)

    y_prompt = _norm(hp, g_final, None, None, 1, 256, F32)
    y_sample = _norm(hs, g_final, None, None, 8, ts, F32)
    return (y_prompt, y_sample, jnp.stack(conv_p), jnp.stack(pool_p),
            jnp.stack(conv_s), jnp.stack(pool_s))
```

```python
import functools

import jax
import jax.numpy as jnp
from jax import lax
from jax.experimental import pallas as pl
from jax.experimental.pallas import tpu as pltpu

EPS = 1e-6
CONV_K = 31
POOL_WINDOWS = (2, 4, 8, 16)
POOL_MAX = max(POOL_WINDOWS)
PAST_LEN = 2048
HALO = 32
LANES = 128
ROW_STRIDE = 4
VMEM_LIMIT = 56 * 1024 * 1024

BF16 = jnp.bfloat16
F32 = jnp.float32


def _params(n_axes, vmem=VMEM_LIMIT):
    return pltpu.CompilerParams(dimension_semantics=("arbitrary",) * n_axes,
                                vmem_limit_bytes=vmem)


def _dot(a, b):
    return jnp.dot(a, b, preferred_element_type=F32)


def _sigmoid(x):
    return jax.nn.sigmoid(x)


def _row_resident(block, index_map):
    return pl.BlockSpec(block, index_map, pipeline_mode=pl.Buffered(1))


def _ada_kernel(c_ref, w_ref, b_ref, o_ref):
    c = c_ref[...]
    s = (c * _sigmoid(c)).astype(BF16)
    o_ref[...] = _dot(s, w_ref[...].astype(BF16)) + b_ref[...]


def _ada(c, w, b, tn=512):
    rows, d = c.shape
    n = w.shape[1]
    return pl.pallas_call(
        _ada_kernel,
        grid=(n // tn,),
        in_specs=[pl.BlockSpec((rows, d), lambda j: (0, 0)),
                  pl.BlockSpec((d, tn), lambda j: (0, j)),
                  pl.BlockSpec((1, tn), lambda j: (0, j))],
        out_specs=pl.BlockSpec((rows, tn), lambda j: (0, j)),
        out_shape=jax.ShapeDtypeStruct((rows, n), F32),
        compiler_params=_params(1),
        name="ada_proj",
    )(c, w, b.reshape(1, n))


def _norm_mod_kernel(h_ref, g_ref, sc_ref, sh_ref, o_ref):
    x = h_ref[...]
    ms = jnp.mean(x * x, axis=-1, keepdims=True)
    y = x * lax.rsqrt(ms + EPS) * g_ref[...]
    o_ref[...] = (y * (1.0 + sc_ref[...]) + sh_ref[...]).astype(o_ref.dtype)


def _norm_kernel(h_ref, g_ref, o_ref):
    x = h_ref[...]
    ms = jnp.mean(x * x, axis=-1, keepdims=True)
    o_ref[...] = (x * lax.rsqrt(ms + EPS) * g_ref[...]).astype(o_ref.dtype)


def _norm(h, g, mod, chunks, bb, tt, out_dtype):
    B, T, D = h.shape
    grid = (B // bb, T // tt)
    h_spec = pl.BlockSpec((bb, tt, D), lambda b, t: (b, t, 0))
    g_spec = pl.BlockSpec((1, 1, D), lambda b, t: (0, 0, 0))
    if mod is None:
        kern, in_specs, args = _norm_kernel, [h_spec, g_spec], (h, g.reshape(1, 1, D))
    else:
        sc, sh = chunks
        kern = _norm_mod_kernel
        in_specs = [h_spec, g_spec,
                    pl.BlockSpec((bb, 1, D), lambda b, t: (b, 0, sc)),
                    pl.BlockSpec((bb, 1, D), lambda b, t: (b, 0, sh))]
        args = (h, g.reshape(1, 1, D), mod, mod)
    return pl.pallas_call(
        kern, grid=grid, in_specs=in_specs, out_specs=h_spec,
        out_shape=jax.ShapeDtypeStruct((B, T, D), out_dtype),
        compiler_params=_params(2), name="rmsnorm",
    )(*args)


def _mm_kernel(x_ref, w_ref, o_ref):
    o_ref[...] = _dot(x_ref[...], w_ref[...].astype(BF16)).astype(o_ref.dtype)


def _matmul(x, w, tm, tn, out_dtype):
    R, K = x.shape
    N = w.shape[1]
    return pl.pallas_call(
        _mm_kernel,
        grid=(R // tm, N // tn),
        in_specs=[_row_resident((tm, K), lambda i, j: (i, 0)),
                  pl.BlockSpec((K, tn), lambda i, j: (0, j))],
        out_specs=pl.BlockSpec((tm, tn), lambda i, j: (i, j)),
        out_shape=jax.ShapeDtypeStruct((R, N), out_dtype),
        compiler_params=_params(2), name="in_proj",
    )(x, w)


def _mixer_kernel(za_ref, zg_ref, zp_ref, zah_ref, zgh_ref, zph_ref, hc_ref, hp_ref,
                  cw_ref, cb_ref, lg_ref, lb_ref,
                  a_ref, p_ref, nc_ref, np_ref,
                  cfull, pfull, conv_out, s2buf, s4buf, s8buf, *, pos0, pool_gc):
    bb, tt, C = za_ref.shape
    t = pl.program_id(1)
    nt = pl.num_programs(1)
    n = HALO + tt

    first = t == 0
    a_in = za_ref[...] * _sigmoid(zg_ref[...])
    a_hist = jnp.where(first, hc_ref[...], zah_ref[...] * _sigmoid(zgh_ref[...]))
    pfull[:, 0:HALO, :] = jnp.where(first, hp_ref[...], zph_ref[...])
    pfull[:, HALO:n, :] = zp_ref[...]
    n_slabs = C // LANES
    for c in range(n_slabs):
        cs = slice(c * LANES, (c + 1) * LANES)
        cfull[:, c, 0:HALO, :] = a_hist[:, :, cs]
        cfull[:, c, HALO:n, :] = a_in[:, :, cs]

    rows = min(tt, 64)
    per = rows // ROW_STRIDE
    for c in range(n_slabs):
        cs = slice(c * LANES, (c + 1) * LANES)
        bias = jnp.broadcast_to(cb_ref[:, cs], (per, LANES))
        for bi in range(bb):
            for r in range(tt // rows):
                base = HALO + r * rows
                acc = [bias] * ROW_STRIDE
                for d in range(CONV_K):
                    w = cw_ref[CONV_K - 1 - d:CONV_K - d, cs]
                    for b in range(ROW_STRIDE):
                        x = cfull[bi, c, pl.ds(base + b - d, per, stride=ROW_STRIDE), :]
                        acc[b] = acc[b] + x * w
                for b in range(ROW_STRIDE):
                    conv_out[bi, c, pl.ds(r * rows + b, per, stride=ROW_STRIDE), :] = acc[b]

    x = jnp.concatenate([conv_out[:, c] for c in range(n_slabs)], axis=-1)
    mu = jnp.mean(x, axis=-1, keepdims=True)
    xc = x - mu
    var = jnp.mean(xc * xc, axis=-1, keepdims=True)
    y = xc * lax.rsqrt(var + EPS) * lg_ref[...] + lb_ref[...]
    a_ref[...] = (y * _sigmoid(y)).astype(a_ref.dtype)

    pos = (pos0 + t * tt + lax.broadcasted_iota(jnp.int32, (1, tt, 1), 1)).astype(F32)
    gc = pool_gc
    bufs = (pfull, s2buf, s4buf, s8buf)
    for l, w in enumerate(POOL_WINDOWS):
        d = w // 2
        src = bufs[l]
        if l + 1 < len(POOL_WINDOWS):
            lo = 8 * (l + 1)
            bufs[l + 1][:, lo:n, :] = src[:, lo:n, gc:] + src[:, lo - d:n - d, gc:]
        s = src[:, HALO:n, 0:gc] + src[:, HALO - d:n - d, 0:gc]
        inv = 1.0 / jnp.minimum(jnp.float32(w), pos + 1.0)
        gs = slice(l * gc, (l + 1) * gc)
        p_ref[:, :, gs] = (s * inv - pfull[:, HALO:n, gs]).astype(p_ref.dtype)

    @pl.when(t == nt - 1)
    def _():
        nc_ref[...] = a_in[:, tt - (CONV_K - 1):, :]
        np_ref[...] = pfull[:, n - (POOL_MAX - 1):n, :]


def _mixer(z, hist_conv, hist_pool, conv_w, conv_b, ln_g, ln_b, pos0, bb, tt):
    B, T, _ = z.shape
    C = conv_w.shape[1]
    n_hc, n_hp = hist_conv.shape[1], hist_pool.shape[1]
    hc = jnp.pad(hist_conv, ((0, 0), (HALO - n_hc, 0), (0, 0)))
    hp = jnp.pad(hist_pool, ((0, 0), (HALO - n_hp, 0), (0, 0)))
    r = tt // HALO

    def cur(col):
        return pl.BlockSpec((bb, tt, C), lambda b, t: (b, t, col))

    def halo(col):
        return pl.BlockSpec((bb, HALO, C), lambda b, t: (b, jnp.maximum(t * r - 1, 0), col))

    hist = pl.BlockSpec((bb, HALO, C), lambda b, t: (b, 0, 0))
    vec = pl.BlockSpec((1, C), lambda b, t: (0, 0))
    gc = C // len(POOL_WINDOWS)
    assert POOL_WINDOWS == (2, 4, 8, 16) and HALO - POOL_MAX // 2 >= 8 * (len(POOL_WINDOWS) - 1)
    kern = functools.partial(_mixer_kernel, pos0=pos0, pool_gc=gc)
    return pl.pallas_call(
        kern,
        grid=(B // bb, T // tt),
        in_specs=[cur(0), cur(1), cur(2), halo(0), halo(1), halo(2), hist, hist,
                  pl.BlockSpec((CONV_K, C), lambda b, t: (0, 0)), vec, vec, vec],
        out_specs=[pl.BlockSpec((bb, tt, C), lambda b, t: (b, t, 0)),
                   pl.BlockSpec((bb, tt, C), lambda b, t: (b, t, 0)),
                   pl.BlockSpec((bb, n_hc, C), lambda b, t: (b, 0, 0)),
                   pl.BlockSpec((bb, n_hp, C), lambda b, t: (b, 0, 0))],
        out_shape=[jax.ShapeDtypeStruct((B, T, C), BF16),
                   jax.ShapeDtypeStruct((B, T, C), BF16),
                   jax.ShapeDtypeStruct((B, n_hc, C), F32),
                   jax.ShapeDtypeStruct((B, n_hp, C), F32)],
        scratch_shapes=[pltpu.VMEM((bb, C // LANES, HALO + tt, LANES), F32),
                        pltpu.VMEM((bb, HALO + tt, C), F32),
                        pltpu.VMEM((bb, C // LANES, tt, LANES), F32)]
                       + [pltpu.VMEM((bb, HALO + tt, C - l * gc), F32) for l in (1, 2, 3)],
        compiler_params=_params(2), name="mixer",
    )(z, z, z, z, z, z, hc, hp, conv_w, conv_b.reshape(1, C), ln_g.reshape(1, C),
      ln_b.reshape(1, C))


def _merge_kernel(a_ref, p_ref, wc_ref, wp_ref, ps_ref, ga_ref, gb_ref, o_ref):
    a = _dot(a_ref[...], wc_ref[...].astype(BF16))
    p = _dot(p_ref[...], wp_ref[0].astype(BF16)) * ps_ref[...]
    o_ref[...] = (_sigmoid(ga_ref[...]) * a + _sigmoid(gb_ref[...]) * p).astype(o_ref.dtype)


def _merge(a_act, p, z, w_conv_out, w_pool, pool_scale, tm, tn):
    R, C = a_act.shape
    G, gc, go = w_pool.shape
    D = w_conv_out.shape[1]
    per = go // tn
    ga0 = (z.shape[1] - 2 * D) // tn
    gb0 = ga0 + D // tn
    return pl.pallas_call(
        _merge_kernel,
        grid=(R // tm, D // tn),
        in_specs=[pl.BlockSpec((tm, C), lambda i, j: (i, 0)),
                  pl.BlockSpec((tm, gc), lambda i, j: (i, j // per)),
                  pl.BlockSpec((C, tn), lambda i, j: (0, j)),
                  pl.BlockSpec((1, gc, tn), lambda i, j: (j // per, 0, j % per)),
                  pl.BlockSpec((1, tn), lambda i, j: (0, j)),
                  pl.BlockSpec((tm, tn), lambda i, j: (i, ga0 + j)),
                  pl.BlockSpec((tm, tn), lambda i, j: (i, gb0 + j))],
        out_specs=pl.BlockSpec((tm, tn), lambda i, j: (i, j)),
        out_shape=jax.ShapeDtypeStruct((R, D), BF16),
        compiler_params=_params(2), name="merge",
    )(a_act, p, w_conv_out, w_pool, pool_scale.reshape(1, D), z, z)


def _mm_res_kernel(x_ref, w_ref, h_ref, gate_ref, o_ref):
    acc = _dot(x_ref[...], w_ref[...].astype(BF16))
    o_ref[...] = h_ref[...] + gate_ref[...] * acc.reshape(h_ref.shape)


def _matmul_residual(x, w, h, mod, gate_chunk, bb, tt, tn, name, k_parts=1):
    B, T, D = h.shape
    assert bb == 1 or tt == T
    K = x.shape[1]
    kb = K // k_parts
    assert kb * k_parts == K
    nt = T // tt
    g0 = gate_chunk * (D // tn)
    for kp in range(k_parts):
        h = pl.pallas_call(
            _mm_res_kernel,
            grid=(B // bb, nt, D // tn),
            in_specs=[_row_resident((bb * tt, kb), lambda b, t, j, kp=kp: (b * nt + t, kp)),
                      pl.BlockSpec((kb, tn), lambda b, t, j, kp=kp: (kp, j)),
                      pl.BlockSpec((bb, tt, tn), lambda b, t, j: (b, t, j)),
                      pl.BlockSpec((bb, 1, tn), lambda b, t, j: (b, 0, g0 + j))],
            out_specs=pl.BlockSpec((bb, tt, tn), lambda b, t, j: (b, t, j)),
            out_shape=jax.ShapeDtypeStruct((B, T, D), F32),
            compiler_params=_params(3), name=name,
        )(x, w, h, mod)
    return h


def _ffn_in_kernel(x_ref, wg_ref, wu_ref, o_ref):
    x = x_ref[...]
    g = _dot(x, wg_ref[...].astype(BF16))
    u = _dot(x, wu_ref[...].astype(BF16))
    o_ref[...] = (g * _sigmoid(g) * u).astype(o_ref.dtype)


def _ffn_in(x, w, tm, tn):
    R, K = x.shape
    F = w.shape[1] // 2
    nf = F // tn
    return pl.pallas_call(
        _ffn_in_kernel,
        grid=(R // tm, nf),
        in_specs=[_row_resident((tm, K), lambda i, j: (i, 0)),
                  pl.BlockSpec((K, tn), lambda i, j: (0, j)),
                  pl.BlockSpec((K, tn), lambda i, j: (0, nf + j))],
        out_specs=pl.BlockSpec((tm, tn), lambda i, j: (i, j)),
        out_shape=jax.ShapeDtypeStruct((R, F), BF16),
        compiler_params=_params(2), name="ffn_in",
    )(x, w, w)


def _layer(h, mod, hist_conv, hist_pool, pos0, lw, *, bb, tt, tm_wide, ebb, ett):
    (g_norm1, w_in, conv_w, conv_b, ln_g, ln_b, w_conv_out, w_pool, pool_scale, w_out,
     g_norm2, w_ffn_in, w_ffn_out) = lw
    B, T, D = h.shape
    R = B * T
    tm = bb * tt

    u = _norm(h, g_norm1, mod, (1, 0), ebb, ett, BF16)
    z = _matmul(u.reshape(R, D), w_in, tm_wide, 512, F32)
    a_act, p, new_conv, new_pool = _mixer(z.reshape(B, T, -1), hist_conv, hist_pool, conv_w,
                                          conv_b, ln_g, ln_b, pos0, ebb, ett)
    C = a_act.shape[-1]
    m = _merge(a_act.reshape(R, C), p.reshape(R, C), z, w_conv_out, w_pool, pool_scale, tm, 512)
    h = _matmul_residual(m, w_out, h, mod, 2, bb, tt, 512, "out_proj")

    u2 = _norm(h, g_norm2, mod, (4, 3), ebb, ett, BF16)
    act = _ffn_in(u2.reshape(R, D), w_ffn_in, tm_wide, 256)
    h = _matmul_residual(act, w_ffn_out, h, mod, 5, bb, tt, 512, "ffn_out", k_parts=2)
    return h, new_conv, new_pool


def kernel(x_prompt, x_sample, state_conv, state_pool, c_prompt, c_sample, w_ada, b_ada,
           g_norm1, w_in, conv_w, conv_b, ln_g, ln_b, w_conv_out, w_pool, pool_scale,
           w_out, g_norm2, w_ffn_in, w_ffn_out, g_final):
    depth = w_ada.shape[0]
    bp, _, D = x_prompt.shape
    bs, ts, _ = x_sample.shape

    c_all = jnp.concatenate([c_prompt, c_sample], axis=0)
    rows = c_all.shape[0]
    c_all = jnp.pad(c_all, ((0, -rows % 16), (0, 0)))

    hp, hs = x_prompt, x_sample
    conv_p, pool_p, conv_s, pool_s = [], [], [], []
    for l in range(depth):
        lw = (g_norm1[l], w_in[l], conv_w[l], conv_b[l], ln_g[l], ln_b[l], w_conv_out[l],
              w_pool[l], pool_scale[l], w_out[l], g_norm2[l], w_ffn_in[l], w_ffn_out[l])
        mod = _ada(c_all, w_ada[l], b_ada[l])
        mod_p = mod[:bp].reshape(bp, 1, -1)
        mod_s = mod[bp:bp + bs].reshape(bs, 1, -1)
        zc = jnp.zeros((bp, CONV_K - 1, conv_w.shape[-1]), F32)
        zp = jnp.zeros((bp, POOL_MAX - 1, conv_w.shape[-1]), F32)
        hp, nc, npl = _layer(hp, mod_p, zc, zp, 0, lw,
                             bb=1, tt=1024, tm_wide=2048, ebb=1, ett=256)
        conv_p.append(nc)
        pool_p.append(npl)
        hs, nc, npl = _layer(hs, mod_s, state_conv[l], state_pool[l], PAST_LEN, lw,
                             bb=bs, tt=ts, tm_wide=bs * ts, ebb=4, ett=ts)
        conv_s.append(nc)
        pool_s.append(npl)

    y_prompt = _norm(hp, g_final, None, None, 1, 256, F32)
    y_sample = _norm(hs, g_final, None, None, 8, ts, F32)
    return (y_prompt, y_sample, jnp.stack(conv_p), jnp.stack(pool_p),
            jnp.stack(conv_s), jnp.stack(pool_s))
```

```python
import functools

import jax
import jax.numpy as jnp
from jax import lax
from jax.experimental import pallas as pl
from jax.experimental.pallas import tpu as pltpu

EPS = 1e-6
CONV_K = 31
POOL_WINDOWS = (2, 4, 8, 16)
POOL_MAX = max(POOL_WINDOWS)
PAST_LEN = 2048
HALO = 32
LANES = 128
ROW_STRIDE = 4
VMEM_LIMIT = 56 * 1024 * 1024

BF16 = jnp.bfloat16
F32 = jnp.float32


def _params(n_axes, vmem=VMEM_LIMIT):
    return pltpu.CompilerParams(dimension_semantics=("arbitrary",) * n_axes,
                                vmem_limit_bytes=vmem)


def _dot(a, b):
    return jnp.dot(a, b, preferred_element_type=F32)


def _sigmoid(x):
    return jax.nn.sigmoid(x)


def _row_resident(block, index_map):
    return pl.BlockSpec(block, index_map, pipeline_mode=pl.Buffered(1))


def _ada_kernel(c_ref, w_ref, b_ref, o_ref):
    c = c_ref[...]
    s = (c * _sigmoid(c)).astype(BF16)
    o_ref[...] = _dot(s, w_ref[...].astype(BF16)) + b_ref[...]


def _ada(c, w, b, tn=512):
    rows, d = c.shape
    n = w.shape[1]
    return pl.pallas_call(
        _ada_kernel,
        grid=(n // tn,),
        in_specs=[pl.BlockSpec((rows, d), lambda j: (0, 0)),
                  pl.BlockSpec((d, tn), lambda j: (0, j)),
                  pl.BlockSpec((1, tn), lambda j: (0, j))],
        out_specs=pl.BlockSpec((rows, tn), lambda j: (0, j)),
        out_shape=jax.ShapeDtypeStruct((rows, n), F32),
        compiler_params=_params(1),
        name="ada_proj",
    )(c, w, b.reshape(1, n))


def _norm_kernel(h_ref, g_ref, o_ref):
    x = h_ref[...]
    ms = jnp.mean(x * x, axis=-1, keepdims=True)
    o_ref[...] = (x * lax.rsqrt(ms + EPS) * g_ref[...]).astype(o_ref.dtype)


def _norm(h, g, bb, tt):
    B, T, D = h.shape
    h_spec = pl.BlockSpec((bb, tt, D), lambda b, t: (b, t, 0))
    return pl.pallas_call(
        _norm_kernel, grid=(B // bb, T // tt),
        in_specs=[h_spec, pl.BlockSpec((1, 1, D), lambda b, t: (0, 0, 0))],
        out_specs=h_spec,
        out_shape=jax.ShapeDtypeStruct((B, T, D), h.dtype),
        compiler_params=_params(2), name="rmsnorm",
    )(h, g.reshape(1, 1, D))


def _norm_proj_kernel(h_ref, g_ref, sc_ref, sh_ref, *rest, n_w, n_tiles, n_chunks, epilogue):
    w_refs, o_ref, bufs = rest[:n_w], rest[n_w], rest[n_w + 1:]
    i = pl.program_id(0)
    j = pl.program_id(1)
    rc = bufs[0].shape[0] // n_chunks

    def stage(buf):
        x = h_ref[...]
        ms = jnp.mean(x * x, axis=-1, keepdims=True)
        y = x * lax.rsqrt(ms + EPS) * g_ref[...]
        y = y * (1.0 + sc_ref[...]) + sh_ref[...]
        row0 = pl.multiple_of(j * rc, rc)
        buf[pl.ds(row0, rc), :] = y.reshape(rc, y.shape[-1]).astype(buf.dtype)

    def multiply(buf):
        x = buf[...]
        parts = [_dot(x, w[...].astype(BF16)) for w in w_refs]
        o_ref[...] = epilogue(*parts).astype(o_ref.dtype)

    staging = jnp.logical_and(i < n_tiles, j < n_chunks)
    working = i >= 1
    for parity in (0, 1):
        mine = i % 2 == parity
        fill, use = bufs[parity], bufs[1 - parity]

        @pl.when(mine & staging & working)
        def _():
            multiply(use)
            stage(fill)

        @pl.when(mine & staging & jnp.logical_not(working))
        def _():
            stage(fill)

        @pl.when(mine & jnp.logical_not(staging) & working)
        def _():
            multiply(use)


def _norm_proj(h, g, mod, sc, sh, w, col0s, n_cols, tn, tm, rc, epilogue, out_dtype, name):
    B, T, D = h.shape
    R = B * T
    n_tiles, n_chunks = R // tm, tm // rc
    if T % rc == 0:
        per = T // rc
        cb, ct = 1, rc

        def chunk_idx(c):
            return c // per, c % per
    else:
        assert rc % T == 0
        cb, ct = rc // T, T

        def chunk_idx(c):
            return c, 0

    def chunk(i, j):
        return jnp.minimum(i, n_tiles - 1) * n_chunks + jnp.minimum(j, n_chunks - 1)

    def h_map(i, j):
        b, t = chunk_idx(chunk(i, j))
        return b, t, 0

    def mod_map(k):
        return lambda i, j: (chunk_idx(chunk(i, j))[0], 0, k)

    def col(i, j):
        return jnp.where(i == 0, 0, j)

    kern = functools.partial(_norm_proj_kernel, n_w=len(col0s), n_tiles=n_tiles,
                             n_chunks=n_chunks, epilogue=epilogue)
    return pl.pallas_call(
        kern,
        grid=(n_tiles + 1, n_cols),
        in_specs=[pl.BlockSpec((cb, ct, D), h_map),
                  pl.BlockSpec((1, 1, D), lambda i, j: (0, 0, 0)),
                  pl.BlockSpec((cb, 1, D), mod_map(sc)),
                  pl.BlockSpec((cb, 1, D), mod_map(sh))]
                 + [pl.BlockSpec((D, tn), lambda i, j, c0=c0: (0, c0 + col(i, j))) for c0 in col0s],
        out_specs=pl.BlockSpec((tm, tn), lambda i, j: (jnp.maximum(i - 1, 0), col(i, j))),
        out_shape=jax.ShapeDtypeStruct((R, n_cols * tn), out_dtype),
        scratch_shapes=[pltpu.VMEM((tm, D), BF16), pltpu.VMEM((tm, D), BF16)],
        compiler_params=_params(2), name=name,
    )(h, g.reshape(1, 1, D), mod, mod, *([w] * len(col0s)))


def _swiglu(g, u):
    return g * _sigmoid(g) * u


def _mixer_kernel(za_ref, zg_ref, zp_ref, zah_ref, zgh_ref, zph_ref, hc_ref, hp_ref,
                  cw_ref, cb_ref, lg_ref, lb_ref,
                  a_ref, p_ref, nc_ref, np_ref,
                  cfull, pfull, conv_out, s2buf, s4buf, s8buf, *, pos0, pool_gc):
    bb, tt, C = za_ref.shape
    t = pl.program_id(1)
    nt = pl.num_programs(1)
    n = HALO + tt

    first = t == 0
    a_in = za_ref[...].astype(F32) * _sigmoid(zg_ref[...].astype(F32))
    a_prev = zah_ref[...].astype(F32) * _sigmoid(zgh_ref[...].astype(F32))
    a_hist = jnp.where(first, hc_ref[...], a_prev)
    pfull[:, 0:HALO, :] = jnp.where(first, hp_ref[...], zph_ref[...].astype(F32))
    pfull[:, HALO:n, :] = zp_ref[...].astype(F32)
    n_slabs = C // LANES
    for c in range(n_slabs):
        cs = slice(c * LANES, (c + 1) * LANES)
        cfull[:, c, 0:HALO, :] = a_hist[:, :, cs]
        cfull[:, c, HALO:n, :] = a_in[:, :, cs]

    rows = min(tt, 64)
    per = rows // ROW_STRIDE
    for c in range(n_slabs):
        cs = slice(c * LANES, (c + 1) * LANES)
        bias = jnp.broadcast_to(cb_ref[:, cs], (per, LANES))
        for bi in range(bb):
            for r in range(tt // rows):
                base = HALO + r * rows
                acc = [bias] * ROW_STRIDE
                for d in range(CONV_K):
                    w = cw_ref[CONV_K - 1 - d:CONV_K - d, cs]
                    for b in range(ROW_STRIDE):
                        x = cfull[bi, c, pl.ds(base + b - d, per, stride=ROW_STRIDE), :]
                        acc[b] = acc[b] + x * w
                for b in range(ROW_STRIDE):
                    conv_out[bi, c, pl.ds(r * rows + b, per, stride=ROW_STRIDE), :] = acc[b]

    x = jnp.concatenate([conv_out[:, c] for c in range(n_slabs)], axis=-1)
    mu = jnp.mean(x, axis=-1, keepdims=True)
    xc = x - mu
    var = jnp.mean(xc * xc, axis=-1, keepdims=True)
    y = xc * lax.rsqrt(var + EPS) * lg_ref[...] + lb_ref[...]
    a_ref[...] = (y * _sigmoid(y)).astype(a_ref.dtype)

    pos = (pos0 + t * tt + lax.broadcasted_iota(jnp.int32, (1, tt, 1), 1)).astype(F32)
    gc = pool_gc
    bufs = (pfull, s2buf, s4buf, s8buf)
    for l, w in enumerate(POOL_WINDOWS):
        d = w // 2
        src = bufs[l]
        if l + 1 < len(POOL_WINDOWS):
            lo = 8 * (l + 1)
            bufs[l + 1][:, lo:n, :] = src[:, lo:n, gc:] + src[:, lo - d:n - d, gc:]
        s = src[:, HALO:n, 0:gc] + src[:, HALO - d:n - d, 0:gc]
        inv = 1.0 / jnp.minimum(jnp.float32(w), pos + 1.0)
        gs = slice(l * gc, (l + 1) * gc)
        p_ref[:, :, gs] = (s * inv - pfull[:, HALO:n, gs]).astype(p_ref.dtype)

    @pl.when(t == nt - 1)
    def _():
        nc_ref[...] = a_in[:, tt - (CONV_K - 1):, :]
        np_ref[...] = pfull[:, n - (POOL_MAX - 1):n, :]


def _mixer(z, hist_conv, hist_pool, conv_w, conv_b, ln_g, ln_b, pos0, bb, tt):
    B, T, _ = z.shape
    C = conv_w.shape[1]
    n_hc, n_hp = hist_conv.shape[1], hist_pool.shape[1]
    hc = jnp.pad(hist_conv, ((0, 0), (HALO - n_hc, 0), (0, 0)))
    hp = jnp.pad(hist_pool, ((0, 0), (HALO - n_hp, 0), (0, 0)))
    r = tt // HALO

    def cur(col):
        return pl.BlockSpec((bb, tt, C), lambda b, t: (b, t, col))

    def halo(col):
        return pl.BlockSpec((bb, HALO, C), lambda b, t: (b, jnp.maximum(t * r - 1, 0), col))

    hist = pl.BlockSpec((bb, HALO, C), lambda b, t: (b, 0, 0))
    vec = pl.BlockSpec((1, C), lambda b, t: (0, 0))
    gc = C // len(POOL_WINDOWS)
    assert POOL_WINDOWS == (2, 4, 8, 16) and HALO - POOL_MAX // 2 >= 8 * (len(POOL_WINDOWS) - 1)
    kern = functools.partial(_mixer_kernel, pos0=pos0, pool_gc=gc)
    return pl.pallas_call(
        kern,
        grid=(B // bb, T // tt),
        in_specs=[cur(0), cur(1), cur(2), halo(0), halo(1), halo(2), hist, hist,
                  pl.BlockSpec((CONV_K, C), lambda b, t: (0, 0)), vec, vec, vec],
        out_specs=[pl.BlockSpec((bb, tt, C), lambda b, t: (b, t, 0)),
                   pl.BlockSpec((bb, tt, C), lambda b, t: (b, t, 0)),
                   pl.BlockSpec((bb, n_hc, C), lambda b, t: (b, 0, 0)),
                   pl.BlockSpec((bb, n_hp, C), lambda b, t: (b, 0, 0))],
        out_shape=[jax.ShapeDtypeStruct((B, T, C), BF16),
                   jax.ShapeDtypeStruct((B, T, C), BF16),
                   jax.ShapeDtypeStruct((B, n_hc, C), F32),
                   jax.ShapeDtypeStruct((B, n_hp, C), F32)],
        scratch_shapes=[pltpu.VMEM((bb, C // LANES, HALO + tt, LANES), F32),
                        pltpu.VMEM((bb, HALO + tt, C), F32),
                        pltpu.VMEM((bb, C // LANES, tt, LANES), F32)]
                       + [pltpu.VMEM((bb, HALO + tt, C - l * gc), F32) for l in (1, 2, 3)],
        compiler_params=_params(2), name="mixer",
    )(z, z, z, z, z, z, hc, hp, conv_w, conv_b.reshape(1, C), ln_g.reshape(1, C),
      ln_b.reshape(1, C))


def _merge_kernel(a_ref, p_ref, wc_ref, wp_ref, ps_ref, ga_ref, gb_ref, o_ref):
    a = _dot(a_ref[...], wc_ref[...].astype(BF16))
    p = _dot(p_ref[...], wp_ref[0].astype(BF16)) * ps_ref[...]
    ga = _sigmoid(ga_ref[...].astype(F32))
    gb = _sigmoid(gb_ref[...].astype(F32))
    o_ref[...] = (ga * a + gb * p).astype(o_ref.dtype)


def _merge(a_act, p, z, w_conv_out, w_pool, pool_scale, tm, tn):
    R, C = a_act.shape
    G, gc, go = w_pool.shape
    D = w_conv_out.shape[1]
    per = go // tn
    ga0 = (z.shape[1] - 2 * D) // tn
    gb0 = ga0 + D // tn
    return pl.pallas_call(
        _merge_kernel,
        grid=(R // tm, D // tn),
        in_specs=[_row_resident((tm, C), lambda i, j: (i, 0)),
                  pl.BlockSpec((tm, gc), lambda i, j: (i, j // per)),
                  pl.BlockSpec((C, tn), lambda i, j: (0, j)),
                  pl.BlockSpec((1, gc, tn), lambda i, j: (j // per, 0, j % per)),
                  pl.BlockSpec((1, tn), lambda i, j: (0, j)),
                  pl.BlockSpec((tm, tn), lambda i, j: (i, ga0 + j)),
                  pl.BlockSpec((tm, tn), lambda i, j: (i, gb0 + j))],
        out_specs=pl.BlockSpec((tm, tn), lambda i, j: (i, j)),
        out_shape=jax.ShapeDtypeStruct((R, D), BF16),
        compiler_params=_params(2), name="merge",
    )(a_act, p, w_conv_out, w_pool, pool_scale.reshape(1, D), z, z)


def _mm_res_kernel(x_ref, w_ref, h_ref, gate_ref, o_ref):
    acc = _dot(x_ref[...], w_ref[...].astype(BF16))
    o_ref[...] = h_ref[...] + gate_ref[...] * acc.reshape(h_ref.shape)


def _matmul_residual(x, w, h, mod, gate_chunk, bb, tt, tn, name, k_parts=1, resident=False):
    B, T, D = h.shape
    assert bb == 1 or tt == T
    K = x.shape[1]
    kb = K // k_parts
    assert kb * k_parts == K
    nt = T // tt
    g0 = gate_chunk * (D // tn)
    lhs_spec = _row_resident if resident else pl.BlockSpec
    for kp in range(k_parts):
        h = pl.pallas_call(
            _mm_res_kernel,
            grid=(B // bb, nt, D // tn),
            in_specs=[lhs_spec((bb * tt, kb), lambda b, t, j, kp=kp: (b * nt + t, kp)),
                      pl.BlockSpec((kb, tn), lambda b, t, j, kp=kp: (kp, j)),
                      pl.BlockSpec((bb, tt, tn), lambda b, t, j: (b, t, j)),
                      pl.BlockSpec((bb, 1, tn), lambda b, t, j: (b, 0, g0 + j))],
            out_specs=pl.BlockSpec((bb, tt, tn), lambda b, t, j: (b, t, j)),
            out_shape=jax.ShapeDtypeStruct((B, T, D), F32),
            compiler_params=_params(3), name=name,
        )(x, w, h, mod)
    return h


def _tiles(B, T):
    long_stream = T >= 2048
    return dict(
        tm=1024,
        rc=128,
        mix=(1, 256) if long_stream else (4, T),
        merge=2048 if long_stream else B * T,
        out=(1, 1024) if long_stream else (B, T),
        ffn_out=((1, 2048), 256) if long_stream else ((B, T), 512),
        norm=(1, 256) if long_stream else (8, T),
    )


def _layer(h, mod, hist_conv, hist_pool, pos0, lw):
    (g_norm1, w_in, conv_w, conv_b, ln_g, ln_b, w_conv_out, w_pool, pool_scale, w_out,
     g_norm2, w_ffn_in, w_ffn_out) = lw
    B, T, D = h.shape
    R = B * T
    tl = _tiles(B, T)

    tn = 512
    z = _norm_proj(h, g_norm1, mod, 1, 0, w_in, (0,), w_in.shape[1] // tn, tn, tl["tm"],
                   tl["rc"], lambda a: a, BF16, "in_proj")
    a_act, p, new_conv, new_pool = _mixer(z.reshape(B, T, -1), hist_conv, hist_pool, conv_w,
                                          conv_b, ln_g, ln_b, pos0, *tl["mix"])
    C = a_act.shape[-1]
    m = _merge(a_act.reshape(R, C), p.reshape(R, C), z, w_conv_out, w_pool, pool_scale,
               tl["merge"], 512)
    h = _matmul_residual(m, w_out, h, mod, 2, *tl["out"], 512, "out_proj")

    tf = 256
    nf = w_ffn_in.shape[1] // (2 * tf)
    act = _norm_proj(h, g_norm2, mod, 4, 3, w_ffn_in, (0, nf), nf, tf, tl["tm"], tl["rc"],
                     _swiglu, BF16, "ffn_in")
    (obb, ott), otn = tl["ffn_out"]
    h = _matmul_residual(act, w_ffn_out, h, mod, 5, obb, ott, otn, "ffn_out", k_parts=2,
                         resident=True)
    return h, new_conv, new_pool


def kernel(x_prompt, x_sample, state_conv, state_pool, c_prompt, c_sample, w_ada, b_ada,
           g_norm1, w_in, conv_w, conv_b, ln_g, ln_b, w_conv_out, w_pool, pool_scale,
           w_out, g_norm2, w_ffn_in, w_ffn_out, g_final):
    depth = w_ada.shape[0]
    bp, _, D = x_prompt.shape
    bs, ts, _ = x_sample.shape

    c_all = jnp.concatenate([c_prompt, c_sample], axis=0)
    rows = c_all.shape[0]
    c_all = jnp.pad(c_all, ((0, -rows % 16), (0, 0)))

    hp, hs = x_prompt, x_sample
    conv_p, pool_p, conv_s, pool_s = [], [], [], []
    for l in range(depth):
        lw = (g_norm1[l], w_in[l], conv_w[l], conv_b[l], ln_g[l], ln_b[l], w_conv_out[l],
              w_pool[l], pool_scale[l], w_out[l], g_norm2[l], w_ffn_in[l], w_ffn_out[l])
        mod = _ada(c_all, w_ada[l], b_ada[l])
        mod_p = mod[:bp].reshape(bp, 1, -1)
        mod_s = mod[bp:bp + bs].reshape(bs, 1, -1)
        zc = jnp.zeros((bp, CONV_K - 1, conv_w.shape[-1]), F32)
        zp = jnp.zeros((bp, POOL_MAX - 1, conv_w.shape[-1]), F32)
        hp, nc, npl = _layer(hp, mod_p, zc, zp, 0, lw)
        conv_p.append(nc)
        pool_p.append(npl)
        hs, nc, npl = _layer(hs, mod_s, state_conv[l], state_pool[l], PAST_LEN, lw)
        conv_s.append(nc)
        pool_s.append(npl)

    y_prompt = _norm(hp, g_final, *_tiles(*hp.shape[:2])["norm"])
    y_sample = _norm(hs, g_final, *_tiles(*hs.shape[:2])["norm"])
    return (y_prompt, y_sample, jnp.stack(conv_p), jnp.stack(pool_p),
            jnp.stack(conv_s), jnp.stack(pool_s))
```

```python
import functools

import jax
import jax.numpy as jnp
from jax import lax
from jax.experimental import pallas as pl
from jax.experimental.pallas import tpu as pltpu

EPS = 1e-6
CONV_K = 31
POOL_WINDOWS = (2, 4, 8, 16)
POOL_MAX = max(POOL_WINDOWS)
PAST_LEN = 2048
HALO = 32
LANES = 128
ROW_STRIDE = 4
VMEM_LIMIT = 56 * 1024 * 1024

BF16 = jnp.bfloat16
F32 = jnp.float32


def _params(n_axes, vmem=VMEM_LIMIT):
    return pltpu.CompilerParams(dimension_semantics=("arbitrary",) * n_axes,
                                vmem_limit_bytes=vmem)


def _dot(a, b):
    return jnp.dot(a, b, preferred_element_type=F32)


def _sigmoid(x):
    return jax.nn.sigmoid(x)


def _row_resident(block, index_map):
    return pl.BlockSpec(block, index_map, pipeline_mode=pl.Buffered(1))


def _ada_kernel(c_ref, w_ref, b_ref, o_ref):
    c = c_ref[...]
    s = (c * _sigmoid(c)).astype(BF16)
    o_ref[...] = _dot(s, w_ref[...].astype(BF16)) + b_ref[...]


def _ada(c, w, b, tn=512):
    rows, d = c.shape
    n = w.shape[1]
    return pl.pallas_call(
        _ada_kernel,
        grid=(n // tn,),
        in_specs=[pl.BlockSpec((rows, d), lambda j: (0, 0)),
                  pl.BlockSpec((d, tn), lambda j: (0, j)),
                  pl.BlockSpec((1, tn), lambda j: (0, j))],
        out_specs=pl.BlockSpec((rows, tn), lambda j: (0, j)),
        out_shape=jax.ShapeDtypeStruct((rows, n), F32),
        compiler_params=_params(1),
        name="ada_proj",
    )(c, w, b.reshape(1, n))


def _norm_kernel(h_ref, g_ref, o_ref):
    x = h_ref[...]
    ms = jnp.mean(x * x, axis=-1, keepdims=True)
    o_ref[...] = (x * lax.rsqrt(ms + EPS) * g_ref[...]).astype(o_ref.dtype)


def _norm(h, g, bb, tt):
    B, T, D = h.shape
    h_spec = pl.BlockSpec((bb, tt, D), lambda b, t: (b, t, 0))
    return pl.pallas_call(
        _norm_kernel, grid=(B // bb, T // tt),
        in_specs=[h_spec, pl.BlockSpec((1, 1, D), lambda b, t: (0, 0, 0))],
        out_specs=h_spec,
        out_shape=jax.ShapeDtypeStruct((B, T, D), h.dtype),
        compiler_params=_params(2), name="rmsnorm",
    )(h, g.reshape(1, 1, D))


def _norm_proj_kernel(h_ref, g_ref, sc_ref, sh_ref, *rest, n_w, n_tiles, n_chunks, epilogue,
                      riding):
    w_refs, rest = rest[:n_w], rest[n_w:]
    if riding:
        ride_in, o_ref, ride_out, *bufs = rest
    else:
        (o_ref, *bufs), ride_in, ride_out = rest, None, None
    i = pl.program_id(0)
    j = pl.program_id(1)
    rc = bufs[0].shape[0] // n_chunks

    def stage(buf):
        x = h_ref[...]
        ms = jnp.mean(x * x, axis=-1, keepdims=True)
        y = x * lax.rsqrt(ms + EPS) * g_ref[...]
        y = y * (1.0 + sc_ref[...]) + sh_ref[...]
        row0 = pl.multiple_of(j * rc, rc)
        buf[pl.ds(row0, rc), :] = y.reshape(rc, y.shape[-1]).astype(buf.dtype)

    def multiply(buf):
        x = buf[...]
        parts = [_dot(x, w[...].astype(BF16)) for w in w_refs]
        o_ref[...] = epilogue(*parts).astype(o_ref.dtype)

    def ride():
        if riding:
            ride_out[...] = ride_in[...].astype(ride_out.dtype)

    staging = jnp.logical_and(i < n_tiles, j < n_chunks)
    working = i >= 1
    for parity in (0, 1):
        mine = i % 2 == parity
        fill, use = bufs[parity], bufs[1 - parity]

        @pl.when(mine & staging & working)
        def _():
            multiply(use)
            stage(fill)
            ride()

        @pl.when(mine & staging & jnp.logical_not(working))
        def _():
            stage(fill)
            ride()

        @pl.when(mine & jnp.logical_not(staging) & working)
        def _():
            multiply(use)
            ride()

        if riding:
            @pl.when(mine & jnp.logical_not(staging) & jnp.logical_not(working))
            def _():
                ride()


def _norm_proj(h, g, mod, sc, sh, w, col0s, n_cols, tn, tm, rc, epilogue, out_dtype, name,
               ride=None):
    B, T, D = h.shape
    R = B * T
    n_tiles, n_chunks = R // tm, tm // rc
    if T % rc == 0:
        per = T // rc
        cb, ct = 1, rc

        def chunk_idx(c):
            return c // per, c % per
    else:
        assert rc % T == 0
        cb, ct = rc // T, T

        def chunk_idx(c):
            return c, 0

    def chunk(i, j):
        return jnp.minimum(i, n_tiles - 1) * n_chunks + jnp.minimum(j, n_chunks - 1)

    def h_map(i, j):
        b, t = chunk_idx(chunk(i, j))
        return b, t, 0

    def mod_map(k):
        return lambda i, j: (chunk_idx(chunk(i, j))[0], 0, k)

    def col(i, j):
        return jnp.where(i == 0, 0, j)

    in_specs = ([pl.BlockSpec((cb, ct, D), h_map),
                 pl.BlockSpec((1, 1, D), lambda i, j: (0, 0, 0)),
                 pl.BlockSpec((cb, 1, D), mod_map(sc)),
                 pl.BlockSpec((cb, 1, D), mod_map(sh))]
                + [pl.BlockSpec((D, tn), lambda i, j, c0=c0: (0, c0 + col(i, j))) for c0 in col0s])
    operands = [h, g.reshape(1, 1, D), mod, mod] + [w] * len(col0s)
    out_specs = [pl.BlockSpec((tm, tn), lambda i, j: (jnp.maximum(i - 1, 0), col(i, j)))]
    out_shape = [jax.ShapeDtypeStruct((R, n_cols * tn), out_dtype)]
    if ride is not None:
        r_arr, r_block = ride
        axis = 0 if r_block[1] == r_arr.shape[1] else 1
        n_blocks = r_arr.shape[axis] // r_block[axis]
        assert n_blocks * r_block[axis] == r_arr.shape[axis]
        assert r_block[1 - axis] == r_arr.shape[1 - axis]
        assert n_blocks <= (n_tiles + 1) * n_cols

        def r_map(i, j):
            s = jnp.minimum(i * n_cols + j, n_blocks - 1)
            return (s, 0) if axis == 0 else (0, s)

        in_specs.append(pl.BlockSpec(r_block, r_map))
        operands.append(r_arr)
        out_specs.append(pl.BlockSpec(r_block, r_map))
        out_shape.append(jax.ShapeDtypeStruct(r_arr.shape, BF16))
    kern = functools.partial(_norm_proj_kernel, n_w=len(col0s), n_tiles=n_tiles,
                             n_chunks=n_chunks, epilogue=epilogue, riding=ride is not None)
    outs = pl.pallas_call(
        kern,
        grid=(n_tiles + 1, n_cols),
        in_specs=in_specs, out_specs=out_specs, out_shape=out_shape,
        scratch_shapes=[pltpu.VMEM((tm, D), BF16), pltpu.VMEM((tm, D), BF16)],
        compiler_params=_params(2), name=name,
    )(*operands)
    return outs if ride is not None else outs[0]


def _swiglu(g, u):
    return g * _sigmoid(g) * u


def _mixer_kernel(za_ref, zg_ref, zp_ref, zah_ref, zgh_ref, zph_ref, hc_ref, hp_ref,
                  cw_ref, cb_ref, lg_ref, lb_ref,
                  a_ref, p_ref, nc_ref, np_ref,
                  cfull, pfull, conv_out, s2buf, s4buf, s8buf, *, pos0, pool_gc):
    bb, tt, C = za_ref.shape
    t = pl.program_id(1)
    nt = pl.num_programs(1)
    n = HALO + tt

    first = t == 0
    a_in = za_ref[...].astype(F32) * _sigmoid(zg_ref[...].astype(F32))
    a_prev = zah_ref[...].astype(F32) * _sigmoid(zgh_ref[...].astype(F32))
    a_hist = jnp.where(first, hc_ref[...], a_prev)
    pfull[:, 0:HALO, :] = jnp.where(first, hp_ref[...], zph_ref[...].astype(F32))
    pfull[:, HALO:n, :] = zp_ref[...].astype(F32)
    n_slabs = C // LANES
    for c in range(n_slabs):
        cs = slice(c * LANES, (c + 1) * LANES)
        cfull[:, c, 0:HALO, :] = a_hist[:, :, cs]
        cfull[:, c, HALO:n, :] = a_in[:, :, cs]

    rows = min(tt, 64)
    per = rows // ROW_STRIDE
    for c in range(n_slabs):
        cs = slice(c * LANES, (c + 1) * LANES)
        bias = jnp.broadcast_to(cb_ref[:, cs], (per, LANES))
        for bi in range(bb):
            for r in range(tt // rows):
                base = HALO + r * rows
                acc = [bias] * ROW_STRIDE
                for d in range(CONV_K):
                    w = cw_ref[CONV_K - 1 - d:CONV_K - d, cs]
                    for b in range(ROW_STRIDE):
                        x = cfull[bi, c, pl.ds(base + b - d, per, stride=ROW_STRIDE), :]
                        acc[b] = acc[b] + x * w
                for b in range(ROW_STRIDE):
                    conv_out[bi, c, pl.ds(r * rows + b, per, stride=ROW_STRIDE), :] = acc[b]

    x = jnp.concatenate([conv_out[:, c] for c in range(n_slabs)], axis=-1)
    mu = jnp.mean(x, axis=-1, keepdims=True)
    xc = x - mu
    var = jnp.mean(xc * xc, axis=-1, keepdims=True)
    y = xc * lax.rsqrt(var + EPS) * lg_ref[...] + lb_ref[...]
    a_ref[...] = (y * _sigmoid(y)).astype(a_ref.dtype)

    pos = (pos0 + t * tt + lax.broadcasted_iota(jnp.int32, (1, tt, 1), 1)).astype(F32)
    gc = pool_gc
    bufs = (pfull, s2buf, s4buf, s8buf)
    for l, w in enumerate(POOL_WINDOWS):
        d = w // 2
        src = bufs[l]
        if l + 1 < len(POOL_WINDOWS):
            lo = 8 * (l + 1)
            bufs[l + 1][:, lo:n, :] = src[:, lo:n, gc:] + src[:, lo - d:n - d, gc:]
        s = src[:, HALO:n, 0:gc] + src[:, HALO - d:n - d, 0:gc]
        inv = 1.0 / jnp.minimum(jnp.float32(w), pos + 1.0)
        gs = slice(l * gc, (l + 1) * gc)
        p_ref[:, :, gs] = (s * inv - pfull[:, HALO:n, gs]).astype(p_ref.dtype)

    @pl.when(t == nt - 1)
    def _():
        nc_ref[...] = a_in[:, tt - (CONV_K - 1):, :]
        np_ref[...] = pfull[:, n - (POOL_MAX - 1):n, :]


def _mixer(z, hist_conv, hist_pool, conv_w, conv_b, ln_g, ln_b, pos0, bb, tt):
    B, T, _ = z.shape
    C = conv_w.shape[1]
    n_hc, n_hp = hist_conv.shape[1], hist_pool.shape[1]
    hc = jnp.pad(hist_conv, ((0, 0), (HALO - n_hc, 0), (0, 0)))
    hp = jnp.pad(hist_pool, ((0, 0), (HALO - n_hp, 0), (0, 0)))
    r = tt // HALO

    def cur(col):
        return pl.BlockSpec((bb, tt, C), lambda b, t: (b, t, col))

    def halo(col):
        return pl.BlockSpec((bb, HALO, C), lambda b, t: (b, jnp.maximum(t * r - 1, 0), col))

    hist = pl.BlockSpec((bb, HALO, C), lambda b, t: (b, 0, 0))
    vec = pl.BlockSpec((1, C), lambda b, t: (0, 0))
    gc = C // len(POOL_WINDOWS)
    assert POOL_WINDOWS == (2, 4, 8, 16) and HALO - POOL_MAX // 2 >= 8 * (len(POOL_WINDOWS) - 1)
    kern = functools.partial(_mixer_kernel, pos0=pos0, pool_gc=gc)
    return pl.pallas_call(
        kern,
        grid=(B // bb, T // tt),
        in_specs=[cur(0), cur(1), cur(2), halo(0), halo(1), halo(2), hist, hist,
                  pl.BlockSpec((CONV_K, C), lambda b, t: (0, 0)), vec, vec, vec],
        out_specs=[pl.BlockSpec((bb, tt, C), lambda b, t: (b, t, 0)),
                   pl.BlockSpec((bb, tt, C), lambda b, t: (b, t, 0)),
                   pl.BlockSpec((bb, n_hc, C), lambda b, t: (b, 0, 0)),
                   pl.BlockSpec((bb, n_hp, C), lambda b, t: (b, 0, 0))],
        out_shape=[jax.ShapeDtypeStruct((B, T, C), BF16),
                   jax.ShapeDtypeStruct((B, T, C), BF16),
                   jax.ShapeDtypeStruct((B, n_hc, C), F32),
                   jax.ShapeDtypeStruct((B, n_hp, C), F32)],
        scratch_shapes=[pltpu.VMEM((bb, C // LANES, HALO + tt, LANES), F32),
                        pltpu.VMEM((bb, HALO + tt, C), F32),
                        pltpu.VMEM((bb, C // LANES, tt, LANES), F32)]
                       + [pltpu.VMEM((bb, HALO + tt, C - l * gc), F32) for l in (1, 2, 3)],
        compiler_params=_params(2), name="mixer",
    )(z, z, z, z, z, z, hc, hp, conv_w, conv_b.reshape(1, C), ln_g.reshape(1, C),
      ln_b.reshape(1, C))


def _merge_kernel(a_ref, p_ref, wc_ref, wp_ref, ps_ref, ga_ref, gb_ref, o_ref):
    a = _dot(a_ref[...], wc_ref[...].astype(BF16))
    p = _dot(p_ref[...], wp_ref[0].astype(BF16)) * ps_ref[...]
    ga = _sigmoid(ga_ref[...].astype(F32))
    gb = _sigmoid(gb_ref[...].astype(F32))
    o_ref[...] = (ga * a + gb * p).astype(o_ref.dtype)


def _merge(a_act, p, z, w_conv_out, w_pool, pool_scale, tm, tn):
    R, C = a_act.shape
    G, gc, go = w_pool.shape
    D = w_conv_out.shape[1]
    per = go // tn
    ga0 = (z.shape[1] - 2 * D) // tn
    gb0 = ga0 + D // tn
    return pl.pallas_call(
        _merge_kernel,
        grid=(R // tm, D // tn),
        in_specs=[_row_resident((tm, C), lambda i, j: (i, 0)),
                  pl.BlockSpec((tm, gc), lambda i, j: (i, j // per)),
                  pl.BlockSpec((C, tn), lambda i, j: (0, j)),
                  pl.BlockSpec((1, gc, tn), lambda i, j: (j // per, 0, j % per)),
                  pl.BlockSpec((1, tn), lambda i, j: (0, j)),
                  pl.BlockSpec((tm, tn), lambda i, j: (i, ga0 + j)),
                  pl.BlockSpec((tm, tn), lambda i, j: (i, gb0 + j))],
        out_specs=pl.BlockSpec((tm, tn), lambda i, j: (i, j)),
        out_shape=jax.ShapeDtypeStruct((R, D), BF16),
        compiler_params=_params(2), name="merge",
    )(a_act, p, w_conv_out, w_pool, pool_scale.reshape(1, D), z, z)


def _mm_res_kernel(x_ref, w_ref, h_ref, gate_ref, o_ref):
    acc = _dot(x_ref[...], w_ref[...].astype(BF16))
    o_ref[...] = h_ref[...] + gate_ref[...] * acc.reshape(h_ref.shape)


def _matmul_residual(x, w, h, mod, gate_chunk, bb, tt, tn, name, resident=False):
    B, T, D = h.shape
    assert bb == 1 or tt == T
    K = x.shape[1]
    nt = T // tt
    g0 = gate_chunk * (D // tn)
    lhs_spec = _row_resident if resident else pl.BlockSpec
    return pl.pallas_call(
        _mm_res_kernel,
        grid=(B // bb, nt, D // tn),
        in_specs=[lhs_spec((bb * tt, K), lambda b, t, j: (b * nt + t, 0)),
                  pl.BlockSpec((K, tn), lambda b, t, j: (0, j)),
                  pl.BlockSpec((bb, tt, tn), lambda b, t, j: (b, t, j)),
                  pl.BlockSpec((bb, 1, tn), lambda b, t, j: (b, 0, g0 + j))],
        out_specs=pl.BlockSpec((bb, tt, tn), lambda b, t, j: (b, t, j)),
        out_shape=jax.ShapeDtypeStruct((B, T, D), F32),
        compiler_params=_params(3), name=name,
    )(x, w, h, mod)


def _tiles(B, T):
    long_stream = T >= 2048
    return dict(
        tm_in=1024,
        tm_ffn=2048 if long_stream else B * T,
        rc=128,
        mix=(1, 256) if long_stream else (4, T),
        merge=2048 if long_stream else B * T,
        out=(1, 1024) if long_stream else (B, T),
        norm=(1, 256) if long_stream else (8, T),
    )


TN_IN = 512
TN_FFN = 256
CAST_COLS = 128
CAST_ROWS = 64


def _layer(h, mod, hist_conv, hist_pool, pos0, lw, ffn_bf16=None):
    (g_norm1, w_in, conv_w, conv_b, ln_g, ln_b, w_conv_out, w_pool, pool_scale, w_out,
     g_norm2, w_ffn_in, w_ffn_out) = lw
    B, T, D = h.shape
    R = B * T
    tl = _tiles(B, T)
    make = ffn_bf16 is None
    wi16, wo16 = (None, None) if make else ffn_bf16

    z = _norm_proj(h, g_norm1, mod, 1, 0, w_in, (0,), w_in.shape[1] // TN_IN, TN_IN, tl["tm_in"],
                   tl["rc"], lambda a: a, BF16, "in_proj",
                   ride=(w_ffn_in, (D, CAST_COLS)) if make else None)
    if make:
        z, wi16 = z
    a_act, p, new_conv, new_pool = _mixer(z.reshape(B, T, -1), hist_conv, hist_pool, conv_w,
                                          conv_b, ln_g, ln_b, pos0, *tl["mix"])
    C = a_act.shape[-1]
    m = _merge(a_act.reshape(R, C), p.reshape(R, C), z, w_conv_out, w_pool, pool_scale,
               tl["merge"], TN_IN)
    h = _matmul_residual(m, w_out, h, mod, 2, *tl["out"], TN_IN, "out_proj")

    nf = wi16.shape[1] // (2 * TN_FFN)
    act = _norm_proj(h, g_norm2, mod, 4, 3, wi16, (0, nf), nf, TN_FFN, tl["tm_ffn"], tl["rc"],
                     _swiglu, BF16, "ffn_in",
                     ride=(w_ffn_out, (CAST_ROWS, D)) if make else None)
    if make:
        act, wo16 = act
    h = _matmul_residual(act, wo16, h, mod, 5, *tl["out"], TN_IN, "ffn_out", resident=True)
    return h, new_conv, new_pool, (wi16, wo16)


def kernel(x_prompt, x_sample, state_conv, state_pool, c_prompt, c_sample, w_ada, b_ada,
           g_norm1, w_in, conv_w, conv_b, ln_g, ln_b, w_conv_out, w_pool, pool_scale,
           w_out, g_norm2, w_ffn_in, w_ffn_out, g_final):
    depth = w_ada.shape[0]
    bp = x_prompt.shape[0]
    bs = x_sample.shape[0]

    c_all = jnp.concatenate([c_prompt, c_sample], axis=0)
    rows = c_all.shape[0]
    c_all = jnp.pad(c_all, ((0, -rows % 16), (0, 0)))

    hp, hs = x_prompt, x_sample
    conv_p, pool_p, conv_s, pool_s = [], [], [], []
    for l in range(depth):
        lw = (g_norm1[l], w_in[l], conv_w[l], conv_b[l], ln_g[l], ln_b[l], w_conv_out[l],
              w_pool[l], pool_scale[l], w_out[l], g_norm2[l], w_ffn_in[l], w_ffn_out[l])
        mod = _ada(c_all, w_ada[l], b_ada[l])
        mod_p = mod[:bp].reshape(bp, 1, -1)
        mod_s = mod[bp:bp + bs].reshape(bs, 1, -1)
        zc = jnp.zeros((bp, CONV_K - 1, conv_w.shape[-1]), F32)
        zp = jnp.zeros((bp, POOL_MAX - 1, conv_w.shape[-1]), F32)
        hp, nc, npl, ffn_bf16 = _layer(hp, mod_p, zc, zp, 0, lw)
        conv_p.append(nc)
        pool_p.append(npl)
        hs, nc, npl, _ = _layer(hs, mod_s, state_conv[l], state_pool[l], PAST_LEN, lw, ffn_bf16)
        conv_s.append(nc)
        pool_s.append(npl)

    y_prompt = _norm(hp, g_final, *_tiles(*hp.shape[:2])["norm"])
    y_sample = _norm(hs, g_final, *_tiles(*hs.shape[:2])["norm"])
    return (y_prompt, y_sample, jnp.stack(conv_p), jnp.stack(pool_p),
            jnp.stack(conv_s), jnp.stack(pool_s))
```

```python
import functools

import jax
import jax.numpy as jnp
from jax import lax
from jax.experimental import pallas as pl
from jax.experimental.pallas import tpu as pltpu

EPS = 1e-6
CONV_K = 31
POOL_WINDOWS = (2, 4, 8, 16)
POOL_MAX = max(POOL_WINDOWS)
PAST_LEN = 2048
HALO = 32
LANES = 128
ROW_STRIDE = 4
VMEM_LIMIT = 56 * 1024 * 1024

BF16 = jnp.bfloat16
F32 = jnp.float32


def _params(n_axes, vmem=VMEM_LIMIT):
    return pltpu.CompilerParams(dimension_semantics=("arbitrary",) * n_axes,
                                vmem_limit_bytes=vmem)


def _dot(a, b):
    return jnp.dot(a, b, preferred_element_type=F32)


def _sigmoid(x):
    return jax.nn.sigmoid(x)


def _row_resident(block, index_map):
    return pl.BlockSpec(block, index_map, pipeline_mode=pl.Buffered(1))


def _ada_kernel(c_ref, w_ref, b_ref, o_ref):
    c = c_ref[...]
    s = (c * _sigmoid(c)).astype(BF16)
    o_ref[...] = _dot(s, w_ref[...].astype(BF16)) + b_ref[...]


def _ada(c, w, b, tn=512):
    rows, d = c.shape
    n = w.shape[1]
    return pl.pallas_call(
        _ada_kernel,
        grid=(n // tn,),
        in_specs=[pl.BlockSpec((rows, d), lambda j: (0, 0)),
                  pl.BlockSpec((d, tn), lambda j: (0, j)),
                  pl.BlockSpec((1, tn), lambda j: (0, j))],
        out_specs=pl.BlockSpec((rows, tn), lambda j: (0, j)),
        out_shape=jax.ShapeDtypeStruct((rows, n), F32),
        compiler_params=_params(1),
        name="ada_proj",
    )(c, w, b.reshape(1, n))


def _norm_kernel(h_ref, g_ref, o_ref):
    x = h_ref[...]
    ms = jnp.mean(x * x, axis=-1, keepdims=True)
    o_ref[...] = (x * lax.rsqrt(ms + EPS) * g_ref[...]).astype(o_ref.dtype)


def _norm(h, g, bb, tt):
    B, T, D = h.shape
    h_spec = pl.BlockSpec((bb, tt, D), lambda b, t: (b, t, 0))
    return pl.pallas_call(
        _norm_kernel, grid=(B // bb, T // tt),
        in_specs=[h_spec, pl.BlockSpec((1, 1, D), lambda b, t: (0, 0, 0))],
        out_specs=h_spec,
        out_shape=jax.ShapeDtypeStruct((B, T, D), h.dtype),
        compiler_params=_params(2), name="rmsnorm",
    )(h, g.reshape(1, 1, D))


def _norm_proj_kernel(h_ref, g_ref, sc_ref, sh_ref, *rest, n_w, n_tiles, n_chunks, epilogue,
                      riding):
    w_refs, rest = rest[:n_w], rest[n_w:]
    if riding:
        ride_in, o_ref, ride_out, *bufs = rest
    else:
        (o_ref, *bufs), ride_in, ride_out = rest, None, None
    i = pl.program_id(0)
    j = pl.program_id(1)
    rc = bufs[0].shape[0] // n_chunks

    def stage(buf):
        x = h_ref[...]
        ms = jnp.mean(x * x, axis=-1, keepdims=True)
        y = x * lax.rsqrt(ms + EPS) * g_ref[...]
        y = y * (1.0 + sc_ref[...]) + sh_ref[...]
        row0 = pl.multiple_of(j * rc, rc)
        buf[pl.ds(row0, rc), :] = y.reshape(rc, y.shape[-1]).astype(buf.dtype)

    def multiply(buf):
        x = buf[...]
        parts = [_dot(x, w[...].astype(BF16)) for w in w_refs]
        o_ref[...] = epilogue(*parts).astype(o_ref.dtype)

    def ride():
        if riding:
            ride_out[...] = ride_in[...].astype(ride_out.dtype)

    staging = jnp.logical_and(i < n_tiles, j < n_chunks)
    working = i >= 1
    for parity in (0, 1):
        mine = i % 2 == parity
        fill, use = bufs[parity], bufs[1 - parity]

        @pl.when(mine & staging & working)
        def _():
            multiply(use)
            stage(fill)
            ride()

        @pl.when(mine & staging & jnp.logical_not(working))
        def _():
            stage(fill)
            ride()

        @pl.when(mine & jnp.logical_not(staging) & working)
        def _():
            multiply(use)
            ride()

        if riding:
            @pl.when(mine & jnp.logical_not(staging) & jnp.logical_not(working))
            def _():
                ride()


def _norm_proj(h, g, mod, sc, sh, w, col0s, n_cols, tn, tm, rc, epilogue, out_dtype, name,
               ride=None):
    B, T, D = h.shape
    R = B * T
    n_tiles, n_chunks = R // tm, tm // rc
    if T % rc == 0:
        per = T // rc
        cb, ct = 1, rc

        def chunk_idx(c):
            return c // per, c % per
    else:
        assert rc % T == 0
        cb, ct = rc // T, T

        def chunk_idx(c):
            return c, 0

    def chunk(i, j):
        return jnp.minimum(i, n_tiles - 1) * n_chunks + jnp.minimum(j, n_chunks - 1)

    def h_map(i, j):
        b, t = chunk_idx(chunk(i, j))
        return b, t, 0

    def mod_map(k):
        return lambda i, j: (chunk_idx(chunk(i, j))[0], 0, k)

    def col(i, j):
        return jnp.where(i == 0, 0, j)

    in_specs = ([pl.BlockSpec((cb, ct, D), h_map),
                 pl.BlockSpec((1, 1, D), lambda i, j: (0, 0, 0)),
                 pl.BlockSpec((cb, 1, D), mod_map(sc)),
                 pl.BlockSpec((cb, 1, D), mod_map(sh))]
                + [pl.BlockSpec((D, tn), lambda i, j, c0=c0: (0, c0 + col(i, j))) for c0 in col0s])
    operands = [h, g.reshape(1, 1, D), mod, mod] + [w] * len(col0s)
    out_specs = [pl.BlockSpec((tm, tn), lambda i, j: (jnp.maximum(i - 1, 0), col(i, j)))]
    out_shape = [jax.ShapeDtypeStruct((R, n_cols * tn), out_dtype)]
    if ride is not None:
        r_arr, r_block = ride
        axis = 0 if r_block[1] == r_arr.shape[1] else 1
        n_blocks = r_arr.shape[axis] // r_block[axis]
        assert n_blocks * r_block[axis] == r_arr.shape[axis]
        assert r_block[1 - axis] == r_arr.shape[1 - axis]
        assert n_blocks <= (n_tiles + 1) * n_cols

        def r_map(i, j):
            s = jnp.minimum(i * n_cols + j, n_blocks - 1)
            return (s, 0) if axis == 0 else (0, s)

        in_specs.append(pl.BlockSpec(r_block, r_map))
        operands.append(r_arr)
        out_specs.append(pl.BlockSpec(r_block, r_map))
        out_shape.append(jax.ShapeDtypeStruct(r_arr.shape, BF16))
    kern = functools.partial(_norm_proj_kernel, n_w=len(col0s), n_tiles=n_tiles,
                             n_chunks=n_chunks, epilogue=epilogue, riding=ride is not None)
    outs = pl.pallas_call(
        kern,
        grid=(n_tiles + 1, n_cols),
        in_specs=in_specs, out_specs=out_specs, out_shape=out_shape,
        scratch_shapes=[pltpu.VMEM((tm, D), BF16), pltpu.VMEM((tm, D), BF16)],
        compiler_params=_params(2), name=name,
    )(*operands)
    return outs if ride is not None else outs[0]


def _swiglu(g, u):
    return g * _sigmoid(g) * u


def _mixer_kernel(za_ref, zg_ref, zp_ref, zah_ref, zgh_ref, zph_ref, hc_ref, hp_ref,
                  cw_ref, cb_ref, lg_ref, lb_ref,
                  a_ref, p_ref, nc_ref, np_ref,
                  cfull, pfull, conv_out, s2buf, s4buf, s8buf, *, pos0, pool_gc):
    bb, tt, C = za_ref.shape
    t = pl.program_id(1)
    nt = pl.num_programs(1)
    n = HALO + tt

    first = t == 0
    a_in = za_ref[...].astype(F32) * _sigmoid(zg_ref[...].astype(F32))
    a_prev = zah_ref[...].astype(F32) * _sigmoid(zgh_ref[...].astype(F32))
    a_hist = jnp.where(first, hc_ref[...], a_prev)
    pfull[:, 0:HALO, :] = jnp.where(first, hp_ref[...], zph_ref[...].astype(F32))
    pfull[:, HALO:n, :] = zp_ref[...].astype(F32)
    n_slabs = C // LANES
    for c in range(n_slabs):
        cs = slice(c * LANES, (c + 1) * LANES)
        cfull[:, c, 0:HALO, :] = a_hist[:, :, cs]
        cfull[:, c, HALO:n, :] = a_in[:, :, cs]

    rows = min(tt, 64)
    per = rows // ROW_STRIDE
    for c in range(n_slabs):
        cs = slice(c * LANES, (c + 1) * LANES)
        bias = jnp.broadcast_to(cb_ref[:, cs], (per, LANES))
        for bi in range(bb):
            for r in range(tt // rows):
                base = HALO + r * rows
                acc = [bias] * ROW_STRIDE
                for d in range(CONV_K):
                    w = cw_ref[CONV_K - 1 - d:CONV_K - d, cs]
                    for b in range(ROW_STRIDE):
                        x = cfull[bi, c, pl.ds(base + b - d, per, stride=ROW_STRIDE), :]
                        acc[b] = acc[b] + x * w
                for b in range(ROW_STRIDE):
                    conv_out[bi, c, pl.ds(r * rows + b, per, stride=ROW_STRIDE), :] = acc[b]

    x = jnp.concatenate([conv_out[:, c] for c in range(n_slabs)], axis=-1)
    mu = jnp.mean(x, axis=-1, keepdims=True)
    xc = x - mu
    var = jnp.mean(xc * xc, axis=-1, keepdims=True)
    y = xc * lax.rsqrt(var + EPS) * lg_ref[...] + lb_ref[...]
    a_ref[...] = (y * _sigmoid(y)).astype(a_ref.dtype)

    pos = (pos0 + t * tt + lax.broadcasted_iota(jnp.int32, (1, tt, 1), 1)).astype(F32)
    gc = pool_gc
    bufs = (pfull, s2buf, s4buf, s8buf)
    for l, w in enumerate(POOL_WINDOWS):
        d = w // 2
        src = bufs[l]
        if l + 1 < len(POOL_WINDOWS):
            lo = 8 * (l + 1)
            bufs[l + 1][:, lo:n, :] = src[:, lo:n, gc:] + src[:, lo - d:n - d, gc:]
        s = src[:, HALO:n, 0:gc] + src[:, HALO - d:n - d, 0:gc]
        inv = 1.0 / jnp.minimum(jnp.float32(w), pos + 1.0)
        gs = slice(l * gc, (l + 1) * gc)
        p_ref[:, :, gs] = (s * inv - pfull[:, HALO:n, gs]).astype(p_ref.dtype)

    @pl.when(t == nt - 1)
    def _():
        nc_ref[...] = a_in[:, tt - (CONV_K - 1):, :]
        np_ref[...] = pfull[:, n - (POOL_MAX - 1):n, :]


def _mixer(z, hist_conv, hist_pool, conv_w, conv_b, ln_g, ln_b, pos0, bb, tt):
    B, T, _ = z.shape
    C = conv_w.shape[1]
    n_hc, n_hp = hist_conv.shape[1], hist_pool.shape[1]
    hc = jnp.pad(hist_conv, ((0, 0), (HALO - n_hc, 0), (0, 0)))
    hp = jnp.pad(hist_pool, ((0, 0), (HALO - n_hp, 0), (0, 0)))
    r = tt // HALO

    def cur(col):
        return pl.BlockSpec((bb, tt, C), lambda b, t: (b, t, col))

    def halo(col):
        return pl.BlockSpec((bb, HALO, C), lambda b, t: (b, jnp.maximum(t * r - 1, 0), col))

    hist = pl.BlockSpec((bb, HALO, C), lambda b, t: (b, 0, 0))
    vec = pl.BlockSpec((1, C), lambda b, t: (0, 0))
    gc = C // len(POOL_WINDOWS)
    assert POOL_WINDOWS == (2, 4, 8, 16) and HALO - POOL_MAX // 2 >= 8 * (len(POOL_WINDOWS) - 1)
    kern = functools.partial(_mixer_kernel, pos0=pos0, pool_gc=gc)
    return pl.pallas_call(
        kern,
        grid=(B // bb, T // tt),
        in_specs=[cur(0), cur(1), cur(2), halo(0), halo(1), halo(2), hist, hist,
                  pl.BlockSpec((CONV_K, C), lambda b, t: (0, 0)), vec, vec, vec],
        out_specs=[pl.BlockSpec((bb, tt, C), lambda b, t: (b, t, 0)),
                   pl.BlockSpec((bb, tt, C), lambda b, t: (b, t, 0)),
                   pl.BlockSpec((bb, n_hc, C), lambda b, t: (b, 0, 0)),
                   pl.BlockSpec((bb, n_hp, C), lambda b, t: (b, 0, 0))],
        out_shape=[jax.ShapeDtypeStruct((B, T, C), BF16),
                   jax.ShapeDtypeStruct((B, T, C), BF16),
                   jax.ShapeDtypeStruct((B, n_hc, C), F32),
                   jax.ShapeDtypeStruct((B, n_hp, C), F32)],
        scratch_shapes=[pltpu.VMEM((bb, C // LANES, HALO + tt, LANES), F32),
                        pltpu.VMEM((bb, HALO + tt, C), F32),
                        pltpu.VMEM((bb, C // LANES, tt, LANES), F32)]
                       + [pltpu.VMEM((bb, HALO + tt, C - l * gc), F32) for l in (1, 2, 3)],
        compiler_params=_params(2), name="mixer",
    )(z, z, z, z, z, z, hc, hp, conv_w, conv_b.reshape(1, C), ln_g.reshape(1, C),
      ln_b.reshape(1, C))


def _merge_kernel(a_ref, p_ref, wc_ref, wp_ref, ps_ref, ga_ref, gb_ref, o_ref):
    a = _dot(a_ref[...], wc_ref[...].astype(BF16))
    p = _dot(p_ref[...], wp_ref[0].astype(BF16)) * ps_ref[...]
    ga = _sigmoid(ga_ref[...].astype(F32))
    gb = _sigmoid(gb_ref[...].astype(F32))
    o_ref[...] = (ga * a + gb * p).astype(o_ref.dtype)


def _merge(a_act, p, z, w_conv_out, w_pool, pool_scale, tm, tn):
    R, C = a_act.shape
    G, gc, go = w_pool.shape
    D = w_conv_out.shape[1]
    per = go // tn
    ga0 = (z.shape[1] - 2 * D) // tn
    gb0 = ga0 + D // tn
    return pl.pallas_call(
        _merge_kernel,
        grid=(R // tm, D // tn),
        in_specs=[_row_resident((tm, C), lambda i, j: (i, 0)),
                  pl.BlockSpec((tm, gc), lambda i, j: (i, j // per)),
                  pl.BlockSpec((C, tn), lambda i, j: (0, j)),
                  pl.BlockSpec((1, gc, tn), lambda i, j: (j // per, 0, j % per)),
                  pl.BlockSpec((1, tn), lambda i, j: (0, j)),
                  pl.BlockSpec((tm, tn), lambda i, j: (i, ga0 + j)),
                  pl.BlockSpec((tm, tn), lambda i, j: (i, gb0 + j))],
        out_specs=pl.BlockSpec((tm, tn), lambda i, j: (i, j)),
        out_shape=jax.ShapeDtypeStruct((R, D), BF16),
        compiler_params=_params(2), name="merge",
    )(a_act, p, w_conv_out, w_pool, pool_scale.reshape(1, D), z, z)


def _mm_res_kernel(x_ref, w_ref, h_ref, gate_ref, o_ref):
    acc = _dot(x_ref[...], w_ref[...].astype(BF16))
    o_ref[...] = h_ref[...] + gate_ref[...] * acc.reshape(h_ref.shape)


def _matmul_residual(x, w, h, mod, gate_chunk, bb, tt, tn, name, resident=False):
    B, T, D = h.shape
    assert bb == 1 or tt == T
    K = x.shape[1]
    nt = T // tt
    g0 = gate_chunk * (D // tn)
    lhs_spec = _row_resident if resident else pl.BlockSpec
    return pl.pallas_call(
        _mm_res_kernel,
        grid=(B // bb, nt, D // tn),
        in_specs=[lhs_spec((bb * tt, K), lambda b, t, j: (b * nt + t, 0)),
                  pl.BlockSpec((K, tn), lambda b, t, j: (0, j)),
                  pl.BlockSpec((bb, tt, tn), lambda b, t, j: (b, t, j)),
                  pl.BlockSpec((bb, 1, tn), lambda b, t, j: (b, 0, g0 + j))],
        out_specs=pl.BlockSpec((bb, tt, tn), lambda b, t, j: (b, t, j)),
        out_shape=jax.ShapeDtypeStruct((B, T, D), F32),
        compiler_params=_params(3), name=name,
    )(x, w, h, mod)


def _tiles(B, T):
    long_stream = T >= 2048
    return dict(
        tm=1024,
        rc=128,
        mix=(1, 256) if long_stream else (4, T),
        merge=2048 if long_stream else B * T,
        out=(1, 1024) if long_stream else (B, T),
        norm=(1, 256) if long_stream else (8, T),
    )


TN_IN = 512
TN_FFN = 256
CAST_ROWS = 32


def _layer(h, mod, hist_conv, hist_pool, pos0, lw, w_ffn_out_bf16=None):
    (g_norm1, w_in, conv_w, conv_b, ln_g, ln_b, w_conv_out, w_pool, pool_scale, w_out,
     g_norm2, w_ffn_in, w_ffn_out) = lw
    B, T, D = h.shape
    R = B * T
    tl = _tiles(B, T)
    make = w_ffn_out_bf16 is None

    z = _norm_proj(h, g_norm1, mod, 1, 0, w_in, (0,), w_in.shape[1] // TN_IN, TN_IN, tl["tm"],
                   tl["rc"], lambda a: a, BF16, "in_proj")
    a_act, p, new_conv, new_pool = _mixer(z.reshape(B, T, -1), hist_conv, hist_pool, conv_w,
                                          conv_b, ln_g, ln_b, pos0, *tl["mix"])
    C = a_act.shape[-1]
    m = _merge(a_act.reshape(R, C), p.reshape(R, C), z, w_conv_out, w_pool, pool_scale,
               tl["merge"], TN_IN)
    h = _matmul_residual(m, w_out, h, mod, 2, *tl["out"], TN_IN, "out_proj")

    nf = w_ffn_in.shape[1] // (2 * TN_FFN)
    act = _norm_proj(h, g_norm2, mod, 4, 3, w_ffn_in, (0, nf), nf, TN_FFN, tl["tm"], tl["rc"],
                     _swiglu, BF16, "ffn_in",
                     ride=(w_ffn_out, (CAST_ROWS, D)) if make else None)
    if make:
        act, w_ffn_out_bf16 = act
    h = _matmul_residual(act, w_ffn_out_bf16, h, mod, 5, *tl["out"], TN_IN, "ffn_out",
                         resident=True)
    return h, new_conv, new_pool, w_ffn_out_bf16


def kernel(x_prompt, x_sample, state_conv, state_pool, c_prompt, c_sample, w_ada, b_ada,
           g_norm1, w_in, conv_w, conv_b, ln_g, ln_b, w_conv_out, w_pool, pool_scale,
           w_out, g_norm2, w_ffn_in, w_ffn_out, g_final):
    depth = w_ada.shape[0]
    bp = x_prompt.shape[0]
    bs = x_sample.shape[0]

    c_all = jnp.concatenate([c_prompt, c_sample], axis=0)
    rows = c_all.shape[0]
    c_all = jnp.pad(c_all, ((0, -rows % 16), (0, 0)))

    hp, hs = x_prompt, x_sample
    conv_p, pool_p, conv_s, pool_s = [], [], [], []
    for l in range(depth):
        lw = (g_norm1[l], w_in[l], conv_w[l], conv_b[l], ln_g[l], ln_b[l], w_conv_out[l],
              w_pool[l], pool_scale[l], w_out[l], g_norm2[l], w_ffn_in[l], w_ffn_out[l])
        mod = _ada(c_all, w_ada[l], b_ada[l])
        mod_p = mod[:bp].reshape(bp, 1, -1)
        mod_s = mod[bp:bp + bs].reshape(bs, 1, -1)
        zc = jnp.zeros((bp, CONV_K - 1, conv_w.shape[-1]), F32)
        zp = jnp.zeros((bp, POOL_MAX - 1, conv_w.shape[-1]), F32)
        hp, nc, npl, wo16 = _layer(hp, mod_p, zc, zp, 0, lw)
        conv_p.append(nc)
        pool_p.append(npl)
        hs, nc, npl, _ = _layer(hs, mod_s, state_conv[l], state_pool[l], PAST_LEN, lw, wo16)
        conv_s.append(nc)
        pool_s.append(npl)

    y_prompt = _norm(hp, g_final, *_tiles(*hp.shape[:2])["norm"])
    y_sample = _norm(hs, g_final, *_tiles(*hs.shape[:2])["norm"])
    return (y_prompt, y_sample, jnp.stack(conv_p), jnp.stack(pool_p),
            jnp.stack(conv_s), jnp.stack(pool_s))
```

```python
import functools

import jax
import jax.numpy as jnp
from jax import lax
from jax.experimental import pallas as pl
from jax.experimental.pallas import tpu as pltpu

EPS = 1e-6
CONV_K = 31
POOL_WINDOWS = (2, 4, 8, 16)
POOL_MAX = max(POOL_WINDOWS)
PAST_LEN = 2048
HALO = 32
LANES = 128
ROW_STRIDE = 4
VMEM_LIMIT = 56 * 1024 * 1024

BF16 = jnp.bfloat16
F32 = jnp.float32


def _params(n_axes, vmem=VMEM_LIMIT):
    return pltpu.CompilerParams(dimension_semantics=("arbitrary",) * n_axes,
                                vmem_limit_bytes=vmem)


def _dot(a, b):
    return jnp.dot(a, b, preferred_element_type=F32)


def _sigmoid(x):
    return jax.nn.sigmoid(x)


def _row_resident(block, index_map):
    return pl.BlockSpec(block, index_map, pipeline_mode=pl.Buffered(1))


def _ada_kernel(c_ref, w_ref, b_ref, o_ref):
    c = c_ref[...]
    s = (c * _sigmoid(c)).astype(BF16)
    o_ref[...] = _dot(s, w_ref[...].astype(BF16)) + b_ref[...]


def _ada(c, w, b, tn=512):
    rows, d = c.shape
    n = w.shape[1]
    return pl.pallas_call(
        _ada_kernel,
        grid=(n // tn,),
        in_specs=[pl.BlockSpec((rows, d), lambda j: (0, 0)),
                  pl.BlockSpec((d, tn), lambda j: (0, j)),
                  pl.BlockSpec((1, tn), lambda j: (0, j))],
        out_specs=pl.BlockSpec((rows, tn), lambda j: (0, j)),
        out_shape=jax.ShapeDtypeStruct((rows, n), F32),
        compiler_params=_params(1),
        name="ada_proj",
    )(c, w, b.reshape(1, n))


def _norm_kernel(h_ref, g_ref, o_ref):
    x = h_ref[...]
    ms = jnp.mean(x * x, axis=-1, keepdims=True)
    o_ref[...] = (x * lax.rsqrt(ms + EPS) * g_ref[...]).astype(o_ref.dtype)


def _norm(h, g, bb, tt):
    B, T, D = h.shape
    h_spec = pl.BlockSpec((bb, tt, D), lambda b, t: (b, t, 0))
    return pl.pallas_call(
        _norm_kernel, grid=(B // bb, T // tt),
        in_specs=[h_spec, pl.BlockSpec((1, 1, D), lambda b, t: (0, 0, 0))],
        out_specs=h_spec,
        out_shape=jax.ShapeDtypeStruct((B, T, D), h.dtype),
        compiler_params=_params(2), name="rmsnorm",
    )(h, g.reshape(1, 1, D))


def _norm_proj_kernel(h_ref, g_ref, sc_ref, sh_ref, *rest, n_w, n_tiles, n_chunks, epilogue,
                      riding):
    w_refs, rest = rest[:n_w], rest[n_w:]
    if riding:
        ride_in, o_ref, ride_out, *bufs = rest
    else:
        (o_ref, *bufs), ride_in, ride_out = rest, None, None
    i = pl.program_id(0)
    j = pl.program_id(1)
    rc = bufs[0].shape[0] // n_chunks

    def stage(buf):
        x = h_ref[...]
        ms = jnp.mean(x * x, axis=-1, keepdims=True)
        y = x * lax.rsqrt(ms + EPS) * g_ref[...]
        y = y * (1.0 + sc_ref[...]) + sh_ref[...]
        row0 = pl.multiple_of(j * rc, rc)
        buf[pl.ds(row0, rc), :] = y.reshape(rc, y.shape[-1]).astype(buf.dtype)

    def multiply(buf):
        x = buf[...]
        parts = [_dot(x, w[...].astype(BF16)) for w in w_refs]
        o_ref[...] = epilogue(*parts).astype(o_ref.dtype)

    def ride():
        if riding:
            ride_out[...] = ride_in[...].astype(ride_out.dtype)

    staging = jnp.logical_and(i < n_tiles, j < n_chunks)
    working = i >= 1
    for parity in (0, 1):
        mine = i % 2 == parity
        fill, use = bufs[parity], bufs[1 - parity]

        @pl.when(mine & staging & working)
        def _():
            multiply(use)
            stage(fill)
            ride()

        @pl.when(mine & staging & jnp.logical_not(working))
        def _():
            stage(fill)
            ride()

        @pl.when(mine & jnp.logical_not(staging) & working)
        def _():
            multiply(use)
            ride()

        if riding:
            @pl.when(mine & jnp.logical_not(staging) & jnp.logical_not(working))
            def _():
                ride()


def _norm_proj(h, g, mod, sc, sh, w, col0s, n_cols, tn, tm, rc, epilogue, out_dtype, name,
               ride=None):
    B, T, D = h.shape
    R = B * T
    n_tiles, n_chunks = R // tm, tm // rc
    if T % rc == 0:
        per = T // rc
        cb, ct = 1, rc

        def chunk_idx(c):
            return c // per, c % per
    else:
        assert rc % T == 0
        cb, ct = rc // T, T

        def chunk_idx(c):
            return c, 0

    def chunk(i, j):
        return jnp.minimum(i, n_tiles - 1) * n_chunks + jnp.minimum(j, n_chunks - 1)

    def h_map(i, j):
        b, t = chunk_idx(chunk(i, j))
        return b, t, 0

    def mod_map(k):
        return lambda i, j: (chunk_idx(chunk(i, j))[0], 0, k)

    def col(i, j):
        return jnp.where(i == 0, 0, j)

    in_specs = ([pl.BlockSpec((cb, ct, D), h_map),
                 pl.BlockSpec((1, 1, D), lambda i, j: (0, 0, 0)),
                 pl.BlockSpec((cb, 1, D), mod_map(sc)),
                 pl.BlockSpec((cb, 1, D), mod_map(sh))]
                + [pl.BlockSpec((D, tn), lambda i, j, c0=c0: (0, c0 + col(i, j))) for c0 in col0s])
    operands = [h, g.reshape(1, 1, D), mod, mod] + [w] * len(col0s)
    out_specs = [pl.BlockSpec((tm, tn), lambda i, j: (jnp.maximum(i - 1, 0), col(i, j)))]
    out_shape = [jax.ShapeDtypeStruct((R, n_cols * tn), out_dtype)]
    if ride is not None:
        r_arr, r_block = ride
        axis = 0 if r_block[1] == r_arr.shape[1] else 1
        n_blocks = r_arr.shape[axis] // r_block[axis]
        assert n_blocks * r_block[axis] == r_arr.shape[axis]
        assert r_block[1 - axis] == r_arr.shape[1 - axis]
        assert n_blocks <= (n_tiles + 1) * n_cols

        def r_map(i, j):
            s = jnp.minimum(i * n_cols + j, n_blocks - 1)
            return (s, 0) if axis == 0 else (0, s)

        in_specs.append(pl.BlockSpec(r_block, r_map))
        operands.append(r_arr)
        out_specs.append(pl.BlockSpec(r_block, r_map))
        out_shape.append(jax.ShapeDtypeStruct(r_arr.shape, BF16))
    kern = functools.partial(_norm_proj_kernel, n_w=len(col0s), n_tiles=n_tiles,
                             n_chunks=n_chunks, epilogue=epilogue, riding=ride is not None)
    outs = pl.pallas_call(
        kern,
        grid=(n_tiles + 1, n_cols),
        in_specs=in_specs, out_specs=out_specs, out_shape=out_shape,
        scratch_shapes=[pltpu.VMEM((tm, D), BF16), pltpu.VMEM((tm, D), BF16)],
        compiler_params=_params(2), name=name,
    )(*operands)
    return outs if ride is not None else outs[0]


def _swiglu(g, u):
    return g * _sigmoid(g) * u


def _mixer_kernel(za_ref, zg_ref, zp_ref, zah_ref, zgh_ref, zph_ref, hc_ref, hp_ref,
                  cw_ref, cb_ref, lg_ref, lb_ref,
                  a_ref, p_ref, nc_ref, np_ref,
                  cfull, pfull, conv_out, s2buf, s4buf, s8buf, *, pos0, pool_gc):
    bb, tt, C = za_ref.shape
    t = pl.program_id(1)
    nt = pl.num_programs(1)
    n = HALO + tt

    first = t == 0
    a_in = za_ref[...].astype(F32) * _sigmoid(zg_ref[...].astype(F32))
    a_prev = zah_ref[...].astype(F32) * _sigmoid(zgh_ref[...].astype(F32))
    a_hist = jnp.where(first, hc_ref[...], a_prev)
    pfull[:, 0:HALO, :] = jnp.where(first, hp_ref[...], zph_ref[...].astype(F32))
    pfull[:, HALO:n, :] = zp_ref[...].astype(F32)
    n_slabs = C // LANES
    for c in range(n_slabs):
        cs = slice(c * LANES, (c + 1) * LANES)
        cfull[:, c, 0:HALO, :] = a_hist[:, :, cs]
        cfull[:, c, HALO:n, :] = a_in[:, :, cs]

    rows = min(tt, 64)
    per = rows // ROW_STRIDE
    for c in range(n_slabs):
        cs = slice(c * LANES, (c + 1) * LANES)
        bias = jnp.broadcast_to(cb_ref[:, cs], (per, LANES))
        for bi in range(bb):
            for r in range(tt // rows):
                base = HALO + r * rows
                acc = [bias] * ROW_STRIDE
                for d in range(CONV_K):
                    w = cw_ref[CONV_K - 1 - d:CONV_K - d, cs]
                    for b in range(ROW_STRIDE):
                        x = cfull[bi, c, pl.ds(base + b - d, per, stride=ROW_STRIDE), :]
                        acc[b] = acc[b] + x * w
                for b in range(ROW_STRIDE):
                    conv_out[bi, c, pl.ds(r * rows + b, per, stride=ROW_STRIDE), :] = acc[b]

    x = jnp.concatenate([conv_out[:, c] for c in range(n_slabs)], axis=-1)
    mu = jnp.mean(x, axis=-1, keepdims=True)
    xc = x - mu
    var = jnp.mean(xc * xc, axis=-1, keepdims=True)
    y = xc * lax.rsqrt(var + EPS) * lg_ref[...] + lb_ref[...]
    a_ref[...] = (y * _sigmoid(y)).astype(a_ref.dtype)

    pos = (pos0 + t * tt + lax.broadcasted_iota(jnp.int32, (1, tt, 1), 1)).astype(F32)
    gc = pool_gc
    bufs = (pfull, s2buf, s4buf, s8buf)
    for l, w in enumerate(POOL_WINDOWS):
        d = w // 2
        src = bufs[l]
        if l + 1 < len(POOL_WINDOWS):
            lo = 8 * (l + 1)
            bufs[l + 1][:, lo:n, :] = src[:, lo:n, gc:] + src[:, lo - d:n - d, gc:]
        s = src[:, HALO:n, 0:gc] + src[:, HALO - d:n - d, 0:gc]
        inv = 1.0 / jnp.minimum(jnp.float32(w), pos + 1.0)
        gs = slice(l * gc, (l + 1) * gc)
        p_ref[:, :, gs] = (s * inv - pfull[:, HALO:n, gs]).astype(p_ref.dtype)

    @pl.when(t == nt - 1)
    def _():
        nc_ref[...] = a_in[:, tt - (CONV_K - 1):, :]
        np_ref[...] = pfull[:, n - (POOL_MAX - 1):n, :]


def _mixer(z, hist_conv, hist_pool, conv_w, conv_b, ln_g, ln_b, pos0, bb, tt):
    B, T, _ = z.shape
    C = conv_w.shape[1]
    n_hc, n_hp = hist_conv.shape[1], hist_pool.shape[1]
    hc = jnp.pad(hist_conv, ((0, 0), (HALO - n_hc, 0), (0, 0)))
    hp = jnp.pad(hist_pool, ((0, 0), (HALO - n_hp, 0), (0, 0)))
    r = tt // HALO

    def cur(col):
        return pl.BlockSpec((bb, tt, C), lambda b, t: (b, t, col))

    def halo(col):
        return pl.BlockSpec((bb, HALO, C), lambda b, t: (b, jnp.maximum(t * r - 1, 0), col))

    hist = pl.BlockSpec((bb, HALO, C), lambda b, t: (b, 0, 0))
    vec = pl.BlockSpec((1, C), lambda b, t: (0, 0))
    gc = C // len(POOL_WINDOWS)
    assert POOL_WINDOWS == (2, 4, 8, 16) and HALO - POOL_MAX // 2 >= 8 * (len(POOL_WINDOWS) - 1)
    kern = functools.partial(_mixer_kernel, pos0=pos0, pool_gc=gc)
    return pl.pallas_call(
        kern,
        grid=(B // bb, T // tt),
        in_specs=[cur(0), cur(1), cur(2), halo(0), halo(1), halo(2), hist, hist,
                  pl.BlockSpec((CONV_K, C), lambda b, t: (0, 0)), vec, vec, vec],
        out_specs=[pl.BlockSpec((bb, tt, C), lambda b, t: (b, t, 0)),
                   pl.BlockSpec((bb, tt, C), lambda b, t: (b, t, 0)),
                   pl.BlockSpec((bb, n_hc, C), lambda b, t: (b, 0, 0)),
                   pl.BlockSpec((bb, n_hp, C), lambda b, t: (b, 0, 0))],
        out_shape=[jax.ShapeDtypeStruct((B, T, C), BF16),
                   jax.ShapeDtypeStruct((B, T, C), BF16),
                   jax.ShapeDtypeStruct((B, n_hc, C), F32),
                   jax.ShapeDtypeStruct((B, n_hp, C), F32)],
        scratch_shapes=[pltpu.VMEM((bb, C // LANES, HALO + tt, LANES), F32),
                        pltpu.VMEM((bb, HALO + tt, C), F32),
                        pltpu.VMEM((bb, C // LANES, tt, LANES), F32)]
                       + [pltpu.VMEM((bb, HALO + tt, C - l * gc), F32) for l in (1, 2, 3)],
        compiler_params=_params(2), name="mixer",
    )(z, z, z, z, z, z, hc, hp, conv_w, conv_b.reshape(1, C), ln_g.reshape(1, C),
      ln_b.reshape(1, C))


def _merge_kernel(a_ref, p_ref, wc_ref, wp_ref, ps_ref, ga_ref, gb_ref, o_ref):
    a = _dot(a_ref[...], wc_ref[...].astype(BF16))
    p = _dot(p_ref[...], wp_ref[0].astype(BF16)) * ps_ref[...]
    ga = 0.5 * jnp.tanh(0.5 * ga_ref[...].astype(F32)) + 0.5
    gb = 0.5 * jnp.tanh(0.5 * gb_ref[...].astype(F32)) + 0.5
    o_ref[...] = (ga * a + gb * p).astype(o_ref.dtype)


def _merge(a_act, p, z, w_conv_out, w_pool, pool_scale, tm, tn):
    R, C = a_act.shape
    G, gc, go = w_pool.shape
    D = w_conv_out.shape[1]
    per = go // tn
    ga0 = (z.shape[1] - 2 * D) // tn
    gb0 = ga0 + D // tn
    return pl.pallas_call(
        _merge_kernel,
        grid=(R // tm, D // tn),
        in_specs=[_row_resident((tm, C), lambda i, j: (i, 0)),
                  pl.BlockSpec((tm, gc), lambda i, j: (i, j // per)),
                  pl.BlockSpec((C, tn), lambda i, j: (0, j)),
                  pl.BlockSpec((1, gc, tn), lambda i, j: (j // per, 0, j % per)),
                  pl.BlockSpec((1, tn), lambda i, j: (0, j)),
                  pl.BlockSpec((tm, tn), lambda i, j: (i, ga0 + j)),
                  pl.BlockSpec((tm, tn), lambda i, j: (i, gb0 + j))],
        out_specs=pl.BlockSpec((tm, tn), lambda i, j: (i, j)),
        out_shape=jax.ShapeDtypeStruct((R, D), BF16),
        compiler_params=_params(2), name="merge",
    )(a_act, p, w_conv_out, w_pool, pool_scale.reshape(1, D), z, z)


def _mm_res_kernel(x_ref, w_ref, h_ref, gate_ref, o_ref):
    acc = _dot(x_ref[...], w_ref[...].astype(BF16))
    o_ref[...] = h_ref[...] + gate_ref[...] * acc.reshape(h_ref.shape)


def _matmul_residual(x, w, h, mod, gate_chunk, bb, tt, tn, name, resident=False):
    B, T, D = h.shape
    assert bb == 1 or tt == T
    K = x.shape[1]
    nt = T // tt
    g0 = gate_chunk * (D // tn)
    lhs_spec = _row_resident if resident else pl.BlockSpec
    return pl.pallas_call(
        _mm_res_kernel,
        grid=(B // bb, nt, D // tn),
        in_specs=[lhs_spec((bb * tt, K), lambda b, t, j: (b * nt + t, 0)),
                  pl.BlockSpec((K, tn), lambda b, t, j: (0, j)),
                  pl.BlockSpec((bb, tt, tn), lambda b, t, j: (b, t, j)),
                  pl.BlockSpec((bb, 1, tn), lambda b, t, j: (b, 0, g0 + j))],
        out_specs=pl.BlockSpec((bb, tt, tn), lambda b, t, j: (b, t, j)),
        out_shape=jax.ShapeDtypeStruct((B, T, D), F32),
        compiler_params=_params(3), name=name,
    )(x, w, h, mod)


def _tiles(B, T):
    long_stream = T >= 2048
    return dict(
        tm=1024,
        rc=128,
        mix=(1, 256) if long_stream else (4, T),
        merge=2048 if long_stream else B * T,
        out=(1, 1024) if long_stream else (B, T),
        norm=(1, 512) if long_stream else (16, T),
    )


TN_IN = 512
TN_OUT = 1024
TN_FFN = 256
CAST_ROWS = 32


def _layer(h, mod, hist_conv, hist_pool, pos0, lw, bf16_weights=None):
    (g_norm1, w_in, conv_w, conv_b, ln_g, ln_b, w_conv_out, w_pool, pool_scale, w_out,
     g_norm2, w_ffn_in, w_ffn_out) = lw
    B, T, D = h.shape
    R = B * T
    tl = _tiles(B, T)
    make = bf16_weights is None
    w_out16, w_ffn_out16 = (None, None) if make else bf16_weights

    z = _norm_proj(h, g_norm1, mod, 1, 0, w_in, (0,), w_in.shape[1] // TN_IN, TN_IN, tl["tm"],
                   tl["rc"], lambda a: a, BF16, "in_proj",
                   ride=(w_out, (CAST_ROWS, D)) if make else None)
    if make:
        z, w_out16 = z
    a_act, p, new_conv, new_pool = _mixer(z.reshape(B, T, -1), hist_conv, hist_pool, conv_w,
                                          conv_b, ln_g, ln_b, pos0, *tl["mix"])
    C = a_act.shape[-1]
    m = _merge(a_act.reshape(R, C), p.reshape(R, C), z, w_conv_out, w_pool, pool_scale,
               tl["merge"], TN_IN)
    h = _matmul_residual(m, w_out16, h, mod, 2, *tl["out"], TN_OUT, "out_proj")

    nf = w_ffn_in.shape[1] // (2 * TN_FFN)
    act = _norm_proj(h, g_norm2, mod, 4, 3, w_ffn_in, (0, nf), nf, TN_FFN, tl["tm"], tl["rc"],
                     _swiglu, BF16, "ffn_in",
                     ride=(w_ffn_out, (CAST_ROWS, D)) if make else None)
    if make:
        act, w_ffn_out16 = act
    h = _matmul_residual(act, w_ffn_out16, h, mod, 5, *tl["out"], TN_IN, "ffn_out",
                         resident=True)
    return h, new_conv, new_pool, (w_out16, w_ffn_out16)


def kernel(x_prompt, x_sample, state_conv, state_pool, c_prompt, c_sample, w_ada, b_ada,
           g_norm1, w_in, conv_w, conv_b, ln_g, ln_b, w_conv_out, w_pool, pool_scale,
           w_out, g_norm2, w_ffn_in, w_ffn_out, g_final):
    depth = w_ada.shape[0]
    bp = x_prompt.shape[0]
    bs = x_sample.shape[0]

    c_all = jnp.concatenate([c_prompt, c_sample], axis=0)
    rows = c_all.shape[0]
    c_all = jnp.pad(c_all, ((0, -rows % 16), (0, 0)))

    hp, hs = x_prompt, x_sample
    conv_p, pool_p, conv_s, pool_s = [], [], [], []
    for l in range(depth):
        lw = (g_norm1[l], w_in[l], conv_w[l], conv_b[l], ln_g[l], ln_b[l], w_conv_out[l],
              w_pool[l], pool_scale[l], w_out[l], g_norm2[l], w_ffn_in[l], w_ffn_out[l])
        mod = _ada(c_all, w_ada[l], b_ada[l])
        mod_p = mod[:bp].reshape(bp, 1, -1)
        mod_s = mod[bp:bp + bs].reshape(bs, 1, -1)
        zc = jnp.zeros((bp, CONV_K - 1, conv_w.shape[-1]), F32)
        zp = jnp.zeros((bp, POOL_MAX - 1, conv_w.shape[-1]), F32)
        hp, nc, npl, w16 = _layer(hp, mod_p, zc, zp, 0, lw)
        conv_p.append(nc)
        pool_p.append(npl)
        hs, nc, npl, _ = _layer(hs, mod_s, state_conv[l], state_pool[l], PAST_LEN, lw, w16)
        conv_s.append(nc)
        pool_s.append(npl)

    y_prompt = _norm(hp, g_final, *_tiles(*hp.shape[:2])["norm"])
    y_sample = _norm(hs, g_final, *_tiles(*hs.shape[:2])["norm"])
    return (y_prompt, y_sample, jnp.stack(conv_p), jnp.stack(pool_p),
            jnp.stack(conv_s), jnp.stack(pool_s))
```
